```python
import math
import jax, jax.numpy as jnp
from jax import lax
import numpy as np

D_MODEL = 1024
BATCH = 4
SEQ = 4096
DEPTH = 4

N_MIXERS = 3
D_FF = 2816
LN_EPS = 1e-5
DEEPNORM_ALPHA = (2 * DEPTH) ** 0.25
DEEPNORM_BETA = (8 * DEPTH) ** -0.25
A_HEADS = 8
A_HEAD_DIM = D_MODEL // A_HEADS
MOBA_BLOCK = 256
MOBA_TOPK = 3
MOBA_Q_CHUNK = 32
REL_BUCKETS = 32
REL_MAX_EXACT = REL_BUCKETS // 2
REL_MAX_DIST = 128
POOL_WINDOWS = (2, 4, 8, 16)
POOL_GROUP = D_MODEL // len(POOL_WINDOWS)
C_HEADS = 4
C_HEAD_DIM = D_MODEL // C_HEADS
C_CONV = 4
C_CHUNK = 64
N_A = len(range(0, DEPTH, N_MIXERS))
N_B = len(range(1, DEPTH, N_MIXERS))
N_C = len(range(2, DEPTH, N_MIXERS))

kernel_name = "hybrid_moba_pool_mlstm_macaron_deepnorm"


def layer_norm(x, g, b):
    xf = x.astype(jnp.float32)
    mu = jnp.mean(xf, axis=-1, keepdims=True)
    var = jnp.mean(jnp.square(xf - mu), axis=-1, keepdims=True)
    y = (xf - mu) * lax.rsqrt(var + LN_EPS)
    return (y * g.astype(jnp.float32) + b.astype(jnp.float32)).astype(x.dtype)


def swiglu(x, w_gu, w_down):
    g, u = jnp.split(x @ w_gu, 2, axis=-1)
    return (jax.nn.silu(g) * u) @ w_down


def t5_bucket(dist):
    n = jnp.maximum(dist, 0)
    is_small = n < REL_MAX_EXACT
    nf = jnp.maximum(n, 1).astype(jnp.float32)
    large = REL_MAX_EXACT + (jnp.log(nf / REL_MAX_EXACT) / math.log(REL_MAX_DIST / REL_MAX_EXACT)
                             * (REL_BUCKETS - REL_MAX_EXACT)).astype(jnp.int32)
    large = jnp.minimum(large, REL_BUCKETS - 1)
    return jnp.where(is_small, n, large)


def moba_attention(h, w_in, w_out, rel_bias):
    B, S, _ = h.shape
    H, dh, blk, qcs = A_HEADS, A_HEAD_DIM, MOBA_BLOCK, MOBA_Q_CHUNK
    qkv = (h @ w_in).reshape(B, S, 3, H, dh)
    q, k, v = [jnp.transpose(qkv[:, :, j], (0, 2, 1, 3)) for j in range(3)]
    nb = -(-S // blk)
    pad = nb * blk - S
    kb = jnp.pad(k, ((0, 0), (0, 0), (0, pad), (0, 0))).reshape(B, H, nb, blk, dh)
    vb = jnp.pad(v, ((0, 0), (0, 0), (0, pad), (0, 0))).reshape(B, H, nb, blk, dh)
    q_block = jnp.arange(S) // blk
    n_sel = min(MOBA_TOPK, nb - 1)
    scale = dh ** -0.5
    rel_t = rel_bias.T
    bi = jnp.arange(B)[:, None, None, None]
    hi = jnp.arange(H)[None, :, None, None]
    if n_sel > 0:
        k_mean = jnp.mean(kb.astype(jnp.float32), axis=3)
        gate = jnp.einsum('bhsd,bhnd->bhsn', q.astype(jnp.float32), k_mean)
        past = jnp.arange(nb)[None, :] < q_block[:, None]
        gate = jnp.where(past, gate, -jnp.inf)
        _, top_idx = lax.top_k(gate, n_sel)
        top_valid = top_idx < q_block[:, None]

    def chunk_attend(c):
        start = c * qcs
        qc = lax.dynamic_slice_in_dim(q, start, qcs, axis=2)
        qpos = start + jnp.arange(qcs)
        j = start // blk
        ko = lax.dynamic_index_in_dim(kb, j, axis=2, keepdims=False)
        vo = lax.dynamic_index_in_dim(vb, j, axis=2, keepdims=False)
        kpos = j * blk + jnp.arange(blk)
        dist = qpos[:, None] - kpos[None, :]
        bias_own = jnp.transpose(rel_bias[t5_bucket(dist)], (2, 0, 1))
        s_own = jnp.einsum('bhqd,bhkd->bhqk', qc, ko) * scale + bias_own
        s_own = jnp.where(dist >= 0, s_own, -jnp.inf).astype(jnp.float32)
        if n_sel > 0:
            idx = lax.dynamic_slice_in_dim(top_idx, start, qcs, axis=2)
            valid = lax.dynamic_slice_in_dim(top_valid, start, qcs, axis=2)
            kg = kb[bi, hi, idx]
            vg = vb[bi, hi, idx]
            kpos_sel = idx[..., None] * blk + jnp.arange(blk)
            bucket = t5_bucket(qpos[:, None, None] - kpos_sel)
            s_sel = jnp.einsum('bhqd,bhqnkd->bhqnk', qc, kg) * scale + rel_t[hi[..., None], bucket]
            s_sel = jnp.where(valid[..., None], s_sel, -jnp.inf).astype(jnp.float32)
            logits = jnp.concatenate([s_own, s_sel.reshape(B, H, qcs, n_sel * blk)], axis=-1)
            p = jax.nn.softmax(logits, axis=-1).astype(v.dtype)
            out = (jnp.einsum('bhqk,bhkd->bhqd', p[..., :blk], vo)
                   + jnp.einsum('bhqnk,bhqnkd->bhqd', p[..., blk:].reshape(B, H, qcs, n_sel, blk), vg))
        else:
            p = jax.nn.softmax(s_own, axis=-1).astype(v.dtype)
            out = jnp.einsum('bhqk,bhkd->bhqd', p, vo)
        return out

    outs = lax.map(chunk_attend, jnp.arange(S // qcs))
    o = jnp.transpose(outs, (1, 0, 3, 2, 4)).reshape(B, S, H * dh)
    return o @ w_out


def pool_mixer(h, w_in, w_group, scale, w_out):
    B, S, _ = h.shape
    u = h @ w_in
    counts = jnp.arange(1, S + 1, dtype=jnp.float32)
    pooled = []
    for g, w in enumerate(POOL_WINDOWS):
        ug = u[..., g * POOL_GROUP:(g + 1) * POOL_GROUP].astype(jnp.float32)
        cs = jnp.cumsum(ug, axis=1)
        cs_lag = jnp.pad(cs, ((0, 0), (w, 0), (0, 0)))[:, :S]
        mean = (cs - cs_lag) / jnp.minimum(counts, float(w))[:, None]
        pooled.append(mean - ug)
    p = jnp.stack(pooled, axis=2).astype(h.dtype)
    y = jnp.einsum('bsgc,gcd->bsgd', p, w_group).reshape(B, S, D_MODEL) * scale
    return y @ w_out


def causal_conv(x, w):
    S = x.shape[1]
    xp = jnp.pad(x, ((0, 0), (C_CONV - 1, 0), (0, 0)))
    y = w[0] * xp[:, 0:S]
    for j in range(1, C_CONV):
        y = y + w[j] * xp[:, j:j + S]
    return y


def mlstm_chunkwise(q, k, v, i_pre, log_f):
    B, H, S, dh = q.shape
    L = C_CHUNK
    nc = S // L

    def to_chunks(t):
        return jnp.moveaxis(t.reshape((B, H, nc, L) + t.shape[3:]), 2, 0)

    qc, kc, vc, ic = to_chunks(q), to_chunks(k), to_chunks(v), to_chunks(i_pre)
    bc = jnp.cumsum(to_chunks(log_f), axis=-1)
    causal = jnp.tril(jnp.ones((L, L), dtype=bool))

    def step(carry, xs):
        C, n, m = carry
        qx, kx, vx, ix, bx = xs
        d_intra = jnp.where(causal, bx[..., :, None] - bx[..., None, :] + ix[..., None, :], -jnp.inf)
        m_inter = bx + m[..., None]
        m_t = jnp.maximum(m_inter, jnp.max(d_intra, axis=-1))
        w = jnp.exp(d_intra - m_t[..., None]) * jnp.einsum('bhtd,bhsd->bhts', qx, kx)
        s_inter = jnp.exp(m_inter - m_t)
        num = s_inter[..., None] * jnp.einsum('bhtk,bhkv->bhtv', qx, C) + jnp.einsum('bhts,bhsv->bhtv', w, vx)
        den = s_inter * jnp.einsum('bhtk,bhk->bht', qx, n) + jnp.sum(w, axis=-1)
        h = num / jnp.maximum(jnp.abs(den), jnp.exp(-m_t))[..., None]
        b_last = bx[..., -1]
        g = b_last[..., None] - bx + ix
        m_new = jnp.maximum(b_last + m, jnp.max(g, axis=-1))
        decay = jnp.exp(b_last + m - m_new)
        wk = jnp.exp(g - m_new[..., None])
        C = decay[..., None, None] * C + jnp.einsum('bhsk,bhsv->bhkv', kx * wk[..., None], vx)
        n = decay[..., None] * n + jnp.einsum('bhs,bhsk->bhk', wk, kx)
        return (C, n, m_new), h

    init = (jnp.zeros((B, H, dh, dh), jnp.float32), jnp.zeros((B, H, dh), jnp.float32),
            jnp.zeros((B, H), jnp.float32))
    _, hs = lax.scan(step, init, (qc, kc, vc, ic, bc))
    return jnp.moveaxis(hs, 0, 2).reshape(B, H, S, dh)


def mlstm_mixer(h, w_in, b_gates, conv_w, norm_g, w_out):
    B, S, _ = h.shape
    H, dh, D = C_HEADS, C_HEAD_DIM, D_MODEL
    proj = h @ w_in
    qk = jax.nn.silu(causal_conv(proj[..., :2 * D], conv_w))
    v = proj[..., 2 * D:3 * D]
    o_pre = proj[..., 3 * D:4 * D]
    gates = (proj[..., 4 * D:] + b_gates).astype(jnp.float32)

    def heads(t):
        return jnp.transpose(t.reshape(B, S, H, dh).astype(jnp.float32), (0, 2, 1, 3))

    i_pre = jnp.transpose(gates[..., :H], (0, 2, 1))
    log_f = jnp.transpose(jax.nn.log_sigmoid(gates[..., H:]), (0, 2, 1))
    ht = mlstm_chunkwise(heads(qk[..., :D]), heads(qk[..., D:]) * dh ** -0.5, heads(v), i_pre, log_f)
    ht = jnp.transpose(ht, (0, 2, 1, 3))
    hc = jax.nn.sigmoid(o_pre.astype(jnp.float32)).reshape(B, S, H, dh) * ht
    mu = jnp.mean(hc, axis=-1, keepdims=True)
    var = jnp.mean(jnp.square(hc - mu), axis=-1, keepdims=True)
    hn = ((hc - mu) * lax.rsqrt(var + LN_EPS)).reshape(B, S, D) * norm_g.astype(jnp.float32)
    return hn.astype(h.dtype) @ w_out


def setup_inputs(seed: int = 0) -> dict:
    key = jax.random.key(seed)
    ks = jax.random.split(key, 18)
    D, H, GC = D_MODEL, C_HEADS, POOL_GROUP

    def nrm(k, shape, std):
        return jax.random.normal(k, shape, jnp.float32) * std

    x = nrm(ks[0], (BATCH, SEQ, D), 1.0)
    rel_bias = nrm(ks[1], (REL_BUCKETS, A_HEADS), 0.2)
    ln_g = 1.0 + nrm(ks[2], (DEPTH, 3, D), 0.02)
    ln_b = nrm(ks[3], (DEPTH, 3, D), 0.02)
    ffn_w_gu = nrm(ks[4], (DEPTH, 2, D, 2 * D_FF), D ** -0.5)
    ffn_w_down = nrm(ks[5], (DEPTH, 2, D_FF, D), D_FF ** -0.5 * DEEPNORM_BETA)
    a_w_in = nrm(ks[6], (N_A, D, 3 * D), D ** -0.5)
    a_w_out = nrm(ks[7], (N_A, D, D), D ** -0.5 * DEEPNORM_BETA)
    b_w_in = nrm(ks[8], (N_B, D, D), D ** -0.5)
    b_w_group = nrm(ks[9], (N_B, len(POOL_WINDOWS), GC, GC), GC ** -0.5)
    b_scale = 1.0 + nrm(ks[10], (N_B, D), 0.1)
    b_w_out = nrm(ks[11], (N_B, D, D), D ** -0.5 * DEEPNORM_BETA)
    c_w_in = nrm(ks[12], (N_C, D, 4 * D + 2 * H), D ** -0.5)
    c_b_gates = jnp.concatenate([nrm(ks[13], (N_C, H), 0.1),
                                 jnp.linspace(3.0, 6.0, H, dtype=jnp.float32)[None, :] + nrm(ks[14], (N_C, H), 0.1)],
                                axis=-1)
    c_conv_w = nrm(ks[15], (N_C, C_CONV, 2 * D), C_CONV ** -0.5)
    c_norm_g = 1.0 + nrm(ks[16], (N_C, D), 0.02)
    c_w_out = nrm(ks[17], (N_C, D, D), D ** -0.5 * DEEPNORM_BETA)
    return {"x": x, "rel_bias": rel_bias, "ln_g": ln_g, "ln_b": ln_b,
            "ffn_w_gu": ffn_w_gu, "ffn_w_down": ffn_w_down,
            "a_w_in": a_w_in, "a_w_out": a_w_out,
            "b_w_in": b_w_in, "b_w_group": b_w_group, "b_scale": b_scale, "b_w_out": b_w_out,
            "c_w_in": c_w_in, "c_b_gates": c_b_gates, "c_conv_w": c_conv_w, "c_norm_g": c_norm_g,
            "c_w_out": c_w_out}


def reference(x, rel_bias, ln_g, ln_b, ffn_w_gu, ffn_w_down, a_w_in, a_w_out,
              b_w_in, b_w_group, b_scale, b_w_out,
              c_w_in, c_b_gates, c_conv_w, c_norm_g, c_w_out):
    for i in range(DEPTH):
        x = layer_norm(DEEPNORM_ALPHA * x + 0.5 * swiglu(x, ffn_w_gu[i, 0], ffn_w_down[i, 0]),
                       ln_g[i, 0], ln_b[i, 0])
        kind, j = i % N_MIXERS, i // N_MIXERS
        if kind == 0:
            y = moba_attention(x, a_w_in[j], a_w_out[j], rel_bias)
        elif kind == 1:
            y = pool_mixer(x, b_w_in[j], b_w_group[j], b_scale[j], b_w_out[j])
        else:
            y = mlstm_mixer(x, c_w_in[j], c_b_gates[j], c_conv_w[j], c_norm_g[j], c_w_out[j])
        x = layer_norm(DEEPNORM_ALPHA * x + y, ln_g[i, 1], ln_b[i, 1])
        x = layer_norm(DEEPNORM_ALPHA * x + 0.5 * swiglu(x, ffn_w_gu[i, 1], ffn_w_down[i, 1]),
                       ln_g[i, 2], ln_b[i, 2])
    return x
```

```python
import functools
import math

import numpy as np
import jax
import jax.numpy as jnp
from jax import lax
from jax.experimental import pallas as pl
from jax.experimental.pallas import tpu as pltpu

F32 = jnp.float32
BF16 = jnp.bfloat16

D_MODEL = 1024
DEPTH = 4
N_MIXERS = 3
D_FF = 2816
LN_EPS = 1e-5
ALPHA = (2 * DEPTH) ** 0.25
A_HEADS = 8
A_HEAD_DIM = D_MODEL // A_HEADS
MOBA_BLOCK = 256
MOBA_TOPK = 3
REL_BUCKETS = 32
REL_MAX_EXACT = REL_BUCKETS // 2
REL_MAX_DIST = 128
POOL_WINDOWS = (2, 4, 8, 16)
POOL_GROUP = D_MODEL // len(POOL_WINDOWS)
POOL_HALO = 16
C_HEADS = 4
C_HEAD_DIM = D_MODEL // C_HEADS
C_CONV = 4
C_CHUNK = 256
CONV_HALO = 8
GATE_PAD = 128

NEG = -1e30
V7X_VMEM_LIMIT = 56 * 1024 * 1024
FFN_CHUNK = 256
ROW_TILE = 512

NT_DIMS = (((1,), (1,)), ((), ()))


def _params(*sem):
    return pltpu.CompilerParams(dimension_semantics=sem, vmem_limit_bytes=V7X_VMEM_LIMIT)


def _const_spec(shape):
    nd = len(shape)
    return pl.BlockSpec(shape, lambda *_: (0,) * nd, pipeline_mode=pl.Buffered(1))


def _layer_norm(z, g, b):
    mu = jnp.mean(z, axis=-1, keepdims=True)
    zc = z - mu
    var = jnp.mean(zc * zc, axis=-1, keepdims=True)
    return zc * lax.rsqrt(var + LN_EPS) * g + b


def _sigmoid(x):
    return 1.0 / (1.0 + jnp.exp(-x))


def _dot(a, b):
    return jnp.dot(a, b, preferred_element_type=F32)


def _dot_nt(a, b):
    return lax.dot_general(a, b, NT_DIMS, preferred_element_type=F32)


def _ffn_kernel(x_ref, wgu_ref, wd_ref, g_ref, b_ref, o_ref, xb_ref, h_ref):
    x = x_ref[...]
    xb_ref[...] = x.astype(BF16)
    for c in range(D_FF // FFN_CHUNK):
        lo = c * FFN_CHUNK
        xb = xb_ref[...]
        gate = _dot(xb, wgu_ref[:, lo:lo + FFN_CHUNK])
        up = _dot(xb, wgu_ref[:, D_FF + lo:D_FF + lo + FFN_CHUNK])
        h_ref[:, lo:lo + FFN_CHUNK] = (gate * _sigmoid(gate) * up).astype(BF16)
    y = _dot(h_ref[...], wd_ref[...])
    o_ref[...] = _layer_norm(ALPHA * x + 0.5 * y, g_ref[...], b_ref[...])


def _ffn(x, w_gu, w_down, g, b):
    t = x.shape[0]
    tm = ROW_TILE
    return pl.pallas_call(
        _ffn_kernel,
        grid=(t // tm,),
        in_specs=[
            pl.BlockSpec((tm, D_MODEL), lambda i: (i, 0)),
            _const_spec((D_MODEL, 2 * D_FF)),
            _const_spec((D_FF, D_MODEL)),
            _const_spec((1, D_MODEL)),
            _const_spec((1, D_MODEL)),
        ],
        out_specs=pl.BlockSpec((tm, D_MODEL), lambda i: (i, 0)),
        out_shape=jax.ShapeDtypeStruct((t, D_MODEL), F32),
        scratch_shapes=[pltpu.VMEM((tm, D_MODEL), BF16), pltpu.VMEM((tm, D_FF), BF16)],
        compiler_params=_params("parallel"),
        name="ffn",
    )(x, w_gu, w_down, g, b)


def _proj_kernel(x_ref, w_ref, o_ref):
    o_ref[...] = _dot(x_ref[...].astype(BF16), w_ref[...]).astype(o_ref.dtype)


def _proj(x, w, out_dtype, name):
    t, k = x.shape
    n = w.shape[1]
    tm = ROW_TILE
    return pl.pallas_call(
        _proj_kernel,
        grid=(t // tm,),
        in_specs=[pl.BlockSpec((tm, k), lambda i: (i, 0)), _const_spec((k, n))],
        out_specs=pl.BlockSpec((tm, n), lambda i: (i, 0)),
        out_shape=jax.ShapeDtypeStruct((t, n), out_dtype),
        compiler_params=_params("parallel"),
        name=name,
    )(x, w)


def _proj_res_ln_kernel(a_ref, w_ref, x_ref, g_ref, b_ref, o_ref):
    y = _dot(a_ref[...].astype(BF16), w_ref[...])
    o_ref[...] = _layer_norm(ALPHA * x_ref[...] + y, g_ref[...], b_ref[...])


def _proj_res_ln(a, w, x, g, b, name):
    t = x.shape[0]
    tm = ROW_TILE
    row = lambda i: (i, 0)
    return pl.pallas_call(
        _proj_res_ln_kernel,
        grid=(t // tm,),
        in_specs=[
            pl.BlockSpec((tm, D_MODEL), row),
            _const_spec((D_MODEL, D_MODEL)),
            pl.BlockSpec((tm, D_MODEL), row),
            _const_spec((1, D_MODEL)),
            _const_spec((1, D_MODEL)),
        ],
        out_specs=pl.BlockSpec((tm, D_MODEL), row),
        out_shape=jax.ShapeDtypeStruct((t, D_MODEL), F32),
        compiler_params=_params("parallel"),
        name=name,
    )(a, w, x, g, b)


def _t5_bucket(dist):
    n = jnp.maximum(dist, 0)
    is_small = n < REL_MAX_EXACT
    nf = jnp.maximum(n, 1).astype(F32)
    large = REL_MAX_EXACT + (jnp.log(nf / REL_MAX_EXACT) / math.log(REL_MAX_DIST / REL_MAX_EXACT)
                             * (REL_BUCKETS - REL_MAX_EXACT)).astype(jnp.int32)
    large = jnp.minimum(large, REL_BUCKETS - 1)
    return jnp.where(is_small, n, large)


def _moba_bias_tables(rel_bias):
    blk = MOBA_BLOCK
    dist = jnp.arange(blk)[:, None] - jnp.arange(blk)[None, :]
    own = jnp.transpose(rel_bias[_t5_bucket(dist)], (2, 0, 1))
    own = jnp.where(dist[None] >= 0, own, NEG)
    adj = jnp.transpose(rel_bias[_t5_bucket(dist + blk)], (2, 0, 1))
    return jnp.stack([own, adj], axis=1).astype(F32)


def _moba_kernel(far_ref, q_ref, k_ref, v_ref, bias_ref, o_ref,
                 kmean_ref, sel_ref, m_ref, l_ref, acc_ref, *, nb):
    blk = MOBA_BLOCK
    h = pl.program_id(1)
    i = pl.program_id(2)
    scale = A_HEAD_DIM ** -0.5

    @pl.when(i == 0)
    def _():
        for j in range(nb):
            kb = k_ref[j * blk:(j + 1) * blk, :].astype(F32)
            kmean_ref[j:j + 1, :] = jnp.mean(kb, axis=0, keepdims=True)

    q = q_ref[...]

    gate = _dot_nt(kmean_ref[...].astype(BF16), q)
    jidx = lax.broadcasted_iota(jnp.int32, (nb, blk), 0)
    cnt = jnp.zeros((nb, blk), F32)
    for jp in range(nb - 1):
        row = gate[jp:jp + 1, :]
        beats = (row > gate) | ((row == gate) & (jp < jidx))
        cnt = cnt + jnp.where(beats & (jp < i), 1.0, 0.0)
    sel_t = jnp.where((cnt < MOBA_TOPK) & (jidx < i), 1.0, 0.0)
    adj_t = jnp.sum(jnp.where(jidx == i - 1, sel_t, 0.0), axis=0, keepdims=True)
    pad = jnp.zeros((128 - nb - 8, blk), F32)
    sel_pad = jnp.concatenate([sel_t, jnp.broadcast_to(adj_t, (8, blk)), pad], axis=0)
    sel_ref[...] = jnp.transpose(sel_pad)

    def scores(j0):
        return _dot_nt(q, k_ref[pl.ds(j0, blk), :]) * scale

    def online(s, j0):
        m_prev = m_ref[...]
        m_new = jnp.maximum(m_prev, jnp.max(s, axis=1, keepdims=True))
        a = jnp.exp(m_prev - m_new)
        p = jnp.exp(s - m_new)
        l_ref[...] = a * l_ref[...] + jnp.sum(p, axis=1, keepdims=True)
        acc_ref[...] = a * acc_ref[...] + _dot(p.astype(BF16), v_ref[pl.ds(j0, blk), :])
        m_ref[...] = m_new

    own0 = pl.multiple_of(i * blk, blk)
    s = scores(own0) + bias_ref[0]
    m0 = jnp.max(s, axis=1, keepdims=True)
    p = jnp.exp(s - m0)
    m_ref[...] = m0
    l_ref[...] = jnp.sum(p, axis=1, keepdims=True)
    acc_ref[...] = _dot(p.astype(BF16), v_ref[pl.ds(own0, blk), :])

    @pl.when(i >= 1)
    def _():
        j0 = pl.multiple_of((i - 1) * blk, blk)
        s = scores(j0) + bias_ref[1]
        online(jnp.where(sel_ref[:, nb:nb + 1] > 0.5, s, NEG), j0)

    far = far_ref[h]
    for j in range(nb - 2):
        @pl.when(j < i - 1)
        def _():
            s = scores(j * blk) + far
            online(jnp.where(sel_ref[:, j:j + 1] > 0.5, s, NEG), j * blk)

    o_ref[...] = (acc_ref[...] / l_ref[...]).astype(o_ref.dtype)


def _moba_attention(qkv, bias, far, batch, seq):
    blk = MOBA_BLOCK
    nb = seq // blk
    hh = A_HEADS
    assert seq % blk == 0 and nb + 8 <= 128
    kern = functools.partial(_moba_kernel, nb=nb)
    return pl.pallas_call(
        kern,
        grid=(batch, hh, nb),
        in_specs=[
            pl.BlockSpec(memory_space=pltpu.SMEM),
            pl.BlockSpec((blk, A_HEAD_DIM), lambda b, h, i: (b * nb + i, h)),
            pl.BlockSpec((seq, A_HEAD_DIM), lambda b, h, i: (b, hh + h)),
            pl.BlockSpec((seq, A_HEAD_DIM), lambda b, h, i: (b, 2 * hh + h)),
            pl.BlockSpec((None, 2, blk, blk), lambda b, h, i: (h, 0, 0, 0)),
        ],
        out_specs=pl.BlockSpec((blk, A_HEAD_DIM), lambda b, h, i: (b * nb + i, h)),
        out_shape=jax.ShapeDtypeStruct((batch * seq, D_MODEL), BF16),
        scratch_shapes=[
            pltpu.VMEM((nb, A_HEAD_DIM), F32),
            pltpu.VMEM((blk, 128), F32),
            pltpu.VMEM((blk, 1), F32),
            pltpu.VMEM((blk, 1), F32),
            pltpu.VMEM((blk, A_HEAD_DIM), F32),
        ],
        compiler_params=_params("parallel", "parallel", "arbitrary"),
        name="moba_attn",
    )(far, qkv, qkv, qkv, bias)


def _moba_layer(x, w_in, w_out, rel_bias, g, b, batch, seq):
    assert REL_MAX_DIST <= MOBA_BLOCK
    qkv = _proj(x, w_in, BF16, "moba_qkv")
    bias = _moba_bias_tables(rel_bias)
    far = rel_bias[REL_BUCKETS - 1].astype(F32)
    o = _moba_attention(qkv, bias, far, batch, seq)
    return _proj_res_ln(o, w_out, x, g, b, "moba_out")


def _pool_kernel(x_ref, halo_ref, win_ref, wgrp_ref, scale_ref, wout_ref, g_ref, b_ref, o_ref,
                 ubuf_ref, ybuf_ref, *, tiles_per_seq):
    tm = x_ref.shape[0]
    hl = POOL_HALO
    ti = pl.program_id(0) % tiles_per_seq
    x = x_ref[...]
    u_halo = _dot(halo_ref[...].astype(BF16), win_ref[...])
    ubuf_ref[0:hl, :] = jnp.where(ti == 0, 0.0, u_halo)
    ubuf_ref[hl:hl + tm, :] = _dot(x.astype(BF16), win_ref[...])
    pos = ti * tm + lax.broadcasted_iota(jnp.int32, (tm, POOL_GROUP), 0)
    for gi, w in enumerate(POOL_WINDOWS):
        lo = gi * POOL_GROUP
        u = ubuf_ref[hl:hl + tm, lo:lo + POOL_GROUP]
        ws = u
        for d in range(1, w):
            ws = ws + ubuf_ref[hl - d:hl - d + tm, lo:lo + POOL_GROUP]
        cnt = jnp.minimum(pos + 1, w).astype(F32)
        pooled = ws / cnt - u
        yg = _dot(pooled.astype(BF16), wgrp_ref[gi]) * scale_ref[:, lo:lo + POOL_GROUP]
        ybuf_ref[:, lo:lo + POOL_GROUP] = yg.astype(BF16)
    y = _dot(ybuf_ref[...], wout_ref[...])
    o_ref[...] = _layer_norm(ALPHA * x + y, g_ref[...], b_ref[...])


def _pool_layer(x, w_in, w_group, scale, w_out, g, b, seq):
    t = x.shape[0]
    tm = ROW_TILE
    hl = POOL_HALO
    assert seq % tm == 0 and tm % hl == 0 and max(POOL_WINDOWS) <= hl
    kern = functools.partial(_pool_kernel, tiles_per_seq=seq // tm)
    ng = len(POOL_WINDOWS)
    return pl.pallas_call(
        kern,
        grid=(t // tm,),
        in_specs=[
            pl.BlockSpec((tm, D_MODEL), lambda i: (i, 0)),
            pl.BlockSpec((hl, D_MODEL), lambda i: (jnp.maximum(i * (tm // hl) - 1, 0), 0)),
            _const_spec((D_MODEL, D_MODEL)),
            _const_spec((ng, POOL_GROUP, POOL_GROUP)),
            _const_spec((1, D_MODEL)),
            _const_spec((D_MODEL, D_MODEL)),
            _const_spec((1, D_MODEL)),
            _const_spec((1, D_MODEL)),
        ],
        out_specs=pl.BlockSpec((tm, D_MODEL), lambda i: (i, 0)),
        out_shape=jax.ShapeDtypeStruct((t, D_MODEL), F32),
        scratch_shapes=[pltpu.VMEM((tm + hl, D_MODEL), F32), pltpu.VMEM((tm, D_MODEL), BF16)],
        compiler_params=_params("parallel"),
        name="pool_layer",
    )(x, x, w_in, w_group, scale, w_out, g, b)


def _log_sigmoid(x):
    return jnp.minimum(x, 0.0) - jnp.log1p(jnp.exp(-jnp.abs(x)))


def _mlstm_kernel(x_ref, qk_ref, v_ref, op_ref, wg_ref, wgt_ref, bg_row_ref, bg_col_ref, cw_ref,
                  ng_ref, wout_ref, g_ref, b_ref, o_ref,
                  c_ref, n_ref, m_ref, cprev_ref, cbuf_ref, hn_ref):
    L = C_CHUNK
    dh = C_HEAD_DIM
    nh = C_HEADS
    hl = CONV_HALO

    @pl.when(pl.program_id(1) == 0)
    def _():
        c_ref[...] = jnp.zeros_like(c_ref)
        n_ref[...] = jnp.zeros_like(n_ref)
        m_ref[...] = jnp.zeros_like(m_ref)
        cprev_ref[...] = jnp.zeros_like(cprev_ref)

    x = x_ref[...]
    xb = x.astype(BF16)
    g_col = _dot(xb, wg_ref[...]) + bg_row_ref[...]
    g_row = _dot_nt(wgt_ref[...], xb) + bg_col_ref[...]

    qk_pre = qk_ref[...]
    cbuf_ref[0:hl, :] = cprev_ref[...]
    cbuf_ref[hl:hl + L, :] = qk_pre
    cprev_ref[...] = qk_pre[L - hl:L, :]
    conv = cw_ref[0:1, :] * cbuf_ref[hl - (C_CONV - 1):hl - (C_CONV - 1) + L, :]
    for j in range(1, C_CONV):
        off = hl - (C_CONV - 1) + j
        conv = conv + cw_ref[j:j + 1, :] * cbuf_ref[off:off + L, :]
    qk = conv * _sigmoid(conv)

    r = lax.broadcasted_iota(jnp.int32, (L, L), 0)
    cc = lax.broadcasted_iota(jnp.int32, (L, L), 1)
    lower = cc <= r

    for hd in range(nh):
        i_col = g_col[:, hd:hd + 1]
        i_row = g_row[hd:hd + 1, :]
        lf_col = _log_sigmoid(g_col[:, nh + hd:nh + hd + 1])
        lf_row = _log_sigmoid(g_row[nh + hd:nh + hd + 1, :])
        b_col = jnp.sum(jnp.where(lower, lf_row, 0.0), axis=1, keepdims=True)
        b_row = jnp.sum(jnp.where(r <= cc, lf_col, 0.0), axis=0, keepdims=True)
        b_last = jnp.sum(lf_row, axis=1, keepdims=True)

        q_h = qk[:, hd * dh:(hd + 1) * dh]
        k_h = qk[:, D_MODEL + hd * dh:D_MODEL + (hd + 1) * dh] * (dh ** -0.5)
        qb = q_h.astype(BF16)
        kb = k_h.astype(BF16)
        vb = v_ref[:, hd * dh:(hd + 1) * dh].astype(BF16)
        c_st = c_ref[hd]
        n_st = n_ref[hd]
        m_prev = m_ref[hd]

        d_intra = jnp.where(lower, b_col - b_row + i_row, NEG)
        m_inter = b_col + m_prev
        m_t = jnp.maximum(m_inter, jnp.max(d_intra, axis=1, keepdims=True))
        w = jnp.exp(d_intra - m_t) * _dot_nt(qb, kb)
        s_inter = jnp.exp(m_inter - m_t)
        num = s_inter * _dot(qb, c_st.astype(BF16)) + _dot(w.astype(BF16), vb)
        den = s_inter * jnp.sum(q_h * n_st, axis=1, keepdims=True) + jnp.sum(w, axis=1, keepdims=True)
        ht = num / jnp.maximum(jnp.abs(den), jnp.exp(-m_t))

        gg_col = b_last - b_col + i_col
        gg_row = b_last - b_row + i_row
        m_new = jnp.maximum(b_last + m_prev, jnp.max(gg_row, axis=1, keepdims=True))
        decay = jnp.exp(b_last + m_prev - m_new)
        kw = k_h * jnp.exp(gg_col - m_new)
        c_ref[hd] = decay * c_st + _dot(jnp.transpose(kw).astype(BF16), vb)
        n_ref[hd] = decay * n_st + jnp.sum(kw, axis=0, keepdims=True)
        m_ref[hd] = m_new

        hc = _sigmoid(op_ref[:, hd * dh:(hd + 1) * dh]) * ht
        mu = jnp.mean(hc, axis=1, keepdims=True)
        hcc = hc - mu
        var = jnp.mean(hcc * hcc, axis=1, keepdims=True)
        hn = hcc * lax.rsqrt(var + LN_EPS) * ng_ref[:, hd * dh:(hd + 1) * dh]
        hn_ref[:, hd * dh:(hd + 1) * dh] = hn.astype(BF16)

    y = _dot(hn_ref[...], wout_ref[...])
    o_ref[...] = _layer_norm(ALPHA * x + y, g_ref[...], b_ref[...])


def _mlstm_layer(x, w_in, b_gates, conv_w, norm_g, w_out, g, b, batch, seq):
    t = x.shape[0]
    L = C_CHUNK
    nc = seq // L
    d = D_MODEL
    nh = C_HEADS
    assert seq % L == 0
    proj = _proj(x, w_in[:, :4 * d], F32, "mlstm_proj")
    w_gate = w_in[:, 4 * d:]
    wg = jnp.pad(w_gate, ((0, 0), (0, GATE_PAD - 2 * nh)))
    wgt = jnp.pad(w_gate.T, ((0, 16 - 2 * nh), (0, 0)))
    bg_row = jnp.pad(b_gates, (0, GATE_PAD - 2 * nh))[None, :].astype(F32)
    bg_col = jnp.pad(b_gates, (0, 16 - 2 * nh))[:, None].astype(F32)
    row = lambda bb, c: (bb * nc + c, 0)
    return pl.pallas_call(
        _mlstm_kernel,
        grid=(batch, nc),
        in_specs=[
            pl.BlockSpec((L, d), row),
            pl.BlockSpec((L, 2 * d), row),
            pl.BlockSpec((L, d), lambda bb, c: (bb * nc + c, 2)),
            pl.BlockSpec((L, d), lambda bb, c: (bb * nc + c, 3)),
            _const_spec((d, GATE_PAD)),
            _const_spec((16, d)),
            _const_spec((1, GATE_PAD)),
            _const_spec((16, 1)),
            _const_spec((C_CONV, 2 * d)),
            _const_spec((1, d)),
            _const_spec((d, d)),
            _const_spec((1, d)),
            _const_spec((1, d)),
        ],
        out_specs=pl.BlockSpec((L, d), row),
        out_shape=jax.ShapeDtypeStruct((t, d), F32),
        scratch_shapes=[
            pltpu.VMEM((nh, C_HEAD_DIM, C_HEAD_DIM), F32),
            pltpu.VMEM((nh, 1, C_HEAD_DIM), F32),
            pltpu.VMEM((nh, 1, 1), F32),
            pltpu.VMEM((CONV_HALO, 2 * d), F32),
            pltpu.VMEM((CONV_HALO + L, 2 * d), F32),
            pltpu.VMEM((L, d), BF16),
        ],
        compiler_params=_params("arbitrary", "arbitrary"),
        name="mlstm_layer",
    )(x, proj, proj, proj, wg.astype(BF16), wgt.astype(BF16), bg_row, bg_col, conv_w.astype(F32),
      norm_g[None, :].astype(F32), w_out, g, b)


def kernel(x, rel_bias, ln_g, ln_b, ffn_w_gu, ffn_w_down, a_w_in, a_w_out, b_w_in, b_w_group, b_scale, b_w_out,
           c_w_in, c_b_gates, c_conv_w, c_norm_g, c_w_out):
    batch, seq, d = x.shape
    h = x.reshape(batch * seq, d)
    bf = lambda w: w.astype(BF16)
    for i in range(DEPTH):
        lg = lambda s: ln_g[i, s][None, :]
        lb = lambda s: ln_b[i, s][None, :]
        h = _ffn(h, bf(ffn_w_gu[i, 0]), bf(ffn_w_down[i, 0]), lg(0), lb(0))
        kind, j = i % N_MIXERS, i // N_MIXERS
        if kind == 0:
            h = _moba_layer(h, bf(a_w_in[j]), bf(a_w_out[j]), rel_bias, lg(1), lb(1), batch, seq)
        elif kind == 1:
            h = _pool_layer(h, bf(b_w_in[j]), bf(b_w_group[j]), b_scale[j][None, :], bf(b_w_out[j]),
                            lg(1), lb(1), seq)
        else:
            h = _mlstm_layer(h, bf(c_w_in[j]), c_b_gates[j], c_conv_w[j], c_norm_g[j], bf(c_w_out[j]),
                             lg(1), lb(1), batch, seq)
        h = _ffn(h, bf(ffn_w_gu[i, 1]), bf(ffn_w_down[i, 1]), lg(2), lb(2))
    return h.reshape(batch, seq, d)
```

```python
import functools
import math

import numpy as np
import jax
import jax.numpy as jnp
from jax import lax
from jax.experimental import pallas as pl
from jax.experimental.pallas import tpu as pltpu

F32 = jnp.float32
BF16 = jnp.bfloat16

D_MODEL = 1024
DEPTH = 4
N_MIXERS = 3
D_FF = 2816
LN_EPS = 1e-5
ALPHA = (2 * DEPTH) ** 0.25
A_HEADS = 8
A_HEAD_DIM = D_MODEL // A_HEADS
MOBA_BLOCK = 256
MOBA_TOPK = 3
REL_BUCKETS = 32
REL_MAX_EXACT = REL_BUCKETS // 2
REL_MAX_DIST = 128
POOL_WINDOWS = (2, 4, 8, 16)
POOL_GROUP = D_MODEL // len(POOL_WINDOWS)
POOL_HALO = 16
C_HEADS = 4
C_HEAD_DIM = D_MODEL // C_HEADS
C_CONV = 4
C_CHUNK = 256
CONV_HALO = 8
GATE_PAD = 128

NEG = -1e30
LOG2E = math.log2(math.e)
MOBA_GROUP = 4
MOBA_HEADS_PER_STEP = 2
V7X_VMEM_LIMIT = 56 * 1024 * 1024
FFN_CHUNK = 256
ROW_TILE = 512

NT_DIMS = (((1,), (1,)), ((), ()))


def _params(*sem):
    return pltpu.CompilerParams(dimension_semantics=sem, vmem_limit_bytes=V7X_VMEM_LIMIT)


def _const_spec(shape):
    nd = len(shape)
    return pl.BlockSpec(shape, lambda *_: (0,) * nd, pipeline_mode=pl.Buffered(1))


def _layer_norm(z, g, b):
    mu = jnp.mean(z, axis=-1, keepdims=True)
    zc = z - mu
    var = jnp.mean(zc * zc, axis=-1, keepdims=True)
    return zc * lax.rsqrt(var + LN_EPS) * g + b


def _sigmoid(x):
    return 1.0 / (1.0 + jnp.exp(-x))


def _dot(a, b):
    return jnp.dot(a, b, preferred_element_type=F32)


def _dot_nt(a, b):
    return lax.dot_general(a, b, NT_DIMS, preferred_element_type=F32)


def _ffn_kernel(x_ref, wgu_ref, wd_ref, g_ref, b_ref, o_ref, xb_ref, h_ref):
    x = x_ref[...]
    xb_ref[...] = x.astype(BF16)
    for c in range(D_FF // FFN_CHUNK):
        lo = c * FFN_CHUNK
        xb = xb_ref[...]
        gate = _dot(xb, wgu_ref[:, lo:lo + FFN_CHUNK])
        up = _dot(xb, wgu_ref[:, D_FF + lo:D_FF + lo + FFN_CHUNK])
        h_ref[:, lo:lo + FFN_CHUNK] = (gate * _sigmoid(gate) * up).astype(BF16)
    y = _dot(h_ref[...], wd_ref[...])
    o_ref[...] = _layer_norm(ALPHA * x + 0.5 * y, g_ref[...], b_ref[...])


def _ffn(x, w_gu, w_down, g, b):
    t = x.shape[0]
    tm = ROW_TILE
    return pl.pallas_call(
        _ffn_kernel,
        grid=(t // tm,),
        in_specs=[
            pl.BlockSpec((tm, D_MODEL), lambda i: (i, 0)),
            _const_spec((D_MODEL, 2 * D_FF)),
            _const_spec((D_FF, D_MODEL)),
            _const_spec((1, D_MODEL)),
            _const_spec((1, D_MODEL)),
        ],
        out_specs=pl.BlockSpec((tm, D_MODEL), lambda i: (i, 0)),
        out_shape=jax.ShapeDtypeStruct((t, D_MODEL), F32),
        scratch_shapes=[pltpu.VMEM((tm, D_MODEL), BF16), pltpu.VMEM((tm, D_FF), BF16)],
        compiler_params=_params("parallel"),
        name="ffn",
    )(x, w_gu, w_down, g, b)


def _proj_kernel(x_ref, w_ref, o_ref):
    o_ref[...] = _dot(x_ref[...].astype(BF16), w_ref[...]).astype(o_ref.dtype)


def _proj(x, w, out_dtype, name):
    t, k = x.shape
    n = w.shape[1]
    tm = ROW_TILE
    return pl.pallas_call(
        _proj_kernel,
        grid=(t // tm,),
        in_specs=[pl.BlockSpec((tm, k), lambda i: (i, 0)), _const_spec((k, n))],
        out_specs=pl.BlockSpec((tm, n), lambda i: (i, 0)),
        out_shape=jax.ShapeDtypeStruct((t, n), out_dtype),
        compiler_params=_params("parallel"),
        name=name,
    )(x, w)


def _proj_res_ln_kernel(a_ref, w_ref, x_ref, g_ref, b_ref, o_ref):
    y = _dot(a_ref[...].astype(BF16), w_ref[...])
    o_ref[...] = _layer_norm(ALPHA * x_ref[...] + y, g_ref[...], b_ref[...])


def _proj_res_ln(a, w, x, g, b, name):
    t = x.shape[0]
    tm = ROW_TILE
    row = lambda i: (i, 0)
    return pl.pallas_call(
        _proj_res_ln_kernel,
        grid=(t // tm,),
        in_specs=[
            pl.BlockSpec((tm, D_MODEL), row),
            _const_spec((D_MODEL, D_MODEL)),
            pl.BlockSpec((tm, D_MODEL), row),
            _const_spec((1, D_MODEL)),
            _const_spec((1, D_MODEL)),
        ],
        out_specs=pl.BlockSpec((tm, D_MODEL), row),
        out_shape=jax.ShapeDtypeStruct((t, D_MODEL), F32),
        compiler_params=_params("parallel"),
        name=name,
    )(a, w, x, g, b)


def _t5_bucket(dist):
    n = jnp.maximum(dist, 0)
    is_small = n < REL_MAX_EXACT
    nf = jnp.maximum(n, 1).astype(F32)
    large = REL_MAX_EXACT + (jnp.log(nf / REL_MAX_EXACT) / math.log(REL_MAX_DIST / REL_MAX_EXACT)
                             * (REL_BUCKETS - REL_MAX_EXACT)).astype(jnp.int32)
    large = jnp.minimum(large, REL_BUCKETS - 1)
    return jnp.where(is_small, n, large)


def _moba_bias_tables(rel_bias):
    blk = MOBA_BLOCK
    dist = jnp.arange(blk)[None, :] - jnp.arange(blk)[:, None]

    def lookup(bucket):
        onehot = bucket[None, :, :, None] == jnp.arange(REL_BUCKETS)
        return jnp.sum(jnp.where(onehot, rel_bias.T[:, None, None, :], 0.0), axis=-1)

    own = jnp.where(dist[None] >= 0, lookup(_t5_bucket(dist)) * LOG2E, NEG)
    adj = lookup(_t5_bucket(dist + blk)) * LOG2E
    far = jnp.broadcast_to((rel_bias[REL_BUCKETS - 1] * LOG2E)[:, None, None], own.shape)
    return jnp.stack([own, adj, far], axis=1).astype(F32)


def _moba_qkv_kernel(x_ref, wqk_ref, wvt_ref, qk_ref, vt_ref):
    xb = x_ref[...].astype(BF16)
    qk_ref[...] = _dot(xb, wqk_ref[...]).astype(BF16)
    vt_ref[...] = _dot_nt(wvt_ref[...], xb).astype(BF16)


def _moba_qkv(x, w_qk, w_vt, batch, seq):
    t = x.shape[0]
    tm = ROW_TILE
    tps = seq // tm
    return pl.pallas_call(
        _moba_qkv_kernel,
        grid=(t // tm,),
        in_specs=[
            pl.BlockSpec((tm, D_MODEL), lambda i: (i, 0)),
            _const_spec((D_MODEL, 2 * D_MODEL)),
            _const_spec((D_MODEL, D_MODEL)),
        ],
        out_specs=[
            pl.BlockSpec((tm, 2 * D_MODEL), lambda i: (i, 0)),
            pl.BlockSpec((None, D_MODEL, tm), lambda i: (i // tps, 0, i % tps)),
        ],
        out_shape=[
            jax.ShapeDtypeStruct((t, 2 * D_MODEL), BF16),
            jax.ShapeDtypeStruct((batch, D_MODEL, seq), BF16),
        ],
        compiler_params=_params("parallel"),
        name="moba_qkv",
    )(x, w_qk, w_vt)


def _moba_kernel(q_ref, k_ref, vt_ref, tab_ref, o_ref,
                 kmean_ref, sel_ref, m_ref, l_ref, acc_ref, *, nb):
    blk = MOBA_BLOCK
    dh = A_HEAD_DIM
    i = pl.program_id(2)
    c1 = (dh ** -0.5) * LOG2E

    @pl.when(i == 0)
    def _():
        for hh in range(MOBA_HEADS_PER_STEP):
            for j in range(nb):
                kb = k_ref[j * blk:(j + 1) * blk, hh * dh:(hh + 1) * dh].astype(F32)
                kmean_ref[hh, j:j + 1, :] = jnp.mean(kb, axis=0, keepdims=True)

    jidx = lax.broadcasted_iota(jnp.int32, (nb, blk), 0)
    for hh in range(MOBA_HEADS_PER_STEP):
        q = q_ref[:, hh * dh:(hh + 1) * dh]
        gate = _dot_nt(kmean_ref[hh].astype(BF16), q)
        cnt = jnp.zeros((nb, blk), F32)
        for jp in range(nb - 1):
            row = gate[jp:jp + 1, :]
            beats = (row > gate) | ((row == gate) & (jp < jidx))
            cnt = cnt + jnp.where(beats & (jp < i), 1.0, 0.0)
        chosen = ((cnt < MOBA_TOPK) & (jidx < i)) | (jidx == i)
        sel_ref[hh] = jnp.where(chosen, 1.0, 0.0)
        m_ref[hh] = jnp.full((1, blk), NEG, F32)
        l_ref[hh] = jnp.zeros((1, blk), F32)
        acc_ref[hh] = jnp.zeros((dh, blk), F32)

    gb = MOBA_GROUP
    for g in range(nb // gb):
        @pl.when(g * gb <= i)
        def _():
            for hh in range(MOBA_HEADS_PER_STEP):
                q = q_ref[:, hh * dh:(hh + 1) * dh]
                st = _dot_nt(k_ref[g * gb * blk:(g + 1) * gb * blk, hh * dh:(hh + 1) * dh], q) * c1
                parts = []
                for jj in range(gb):
                    j = g * gb + jj
                    rel = jnp.clip(i - j, 0, 2)
                    tj = st[jj * blk:(jj + 1) * blk, :] + tab_ref[hh, rel]
                    parts.append(jnp.where(sel_ref[hh, j:j + 1, :] > 0.5, tj, NEG))
                t = jnp.concatenate(parts, axis=0)
                m_prev = m_ref[hh]
                m_new = jnp.maximum(m_prev, jnp.max(t, axis=0, keepdims=True))
                a = jnp.exp2(m_prev - m_new)
                p = jnp.exp2(t - m_new)
                l_ref[hh] = a * l_ref[hh] + jnp.sum(p, axis=0, keepdims=True)
                pv = _dot(vt_ref[hh * dh:(hh + 1) * dh, g * gb * blk:(g + 1) * gb * blk], p.astype(BF16))
                acc_ref[hh] = a * acc_ref[hh] + pv
                m_ref[hh] = m_new

    for hh in range(MOBA_HEADS_PER_STEP):
        o_t = acc_ref[hh] / l_ref[hh]
        o_ref[:, hh * dh:(hh + 1) * dh] = jnp.transpose(o_t).astype(o_ref.dtype)


def _moba_attention(qk, vt, tab, batch, seq):
    blk = MOBA_BLOCK
    nb = seq // blk
    hp = MOBA_HEADS_PER_STEP
    w = hp * A_HEAD_DIM
    ngrp = A_HEADS // hp
    assert seq % blk == 0 and nb % MOBA_GROUP == 0 and A_HEADS % hp == 0
    kern = functools.partial(_moba_kernel, nb=nb)
    return pl.pallas_call(
        kern,
        grid=(batch, ngrp, nb),
        in_specs=[
            pl.BlockSpec((blk, w), lambda b, h, i: (b * nb + i, h)),
            pl.BlockSpec((seq, w), lambda b, h, i: (b, ngrp + h)),
            pl.BlockSpec((None, w, seq), lambda b, h, i: (b, h, 0)),
            pl.BlockSpec((hp, 3, blk, blk), lambda b, h, i: (h, 0, 0, 0)),
        ],
        out_specs=pl.BlockSpec((blk, w), lambda b, h, i: (b * nb + i, h)),
        out_shape=jax.ShapeDtypeStruct((batch * seq, D_MODEL), BF16),
        scratch_shapes=[
            pltpu.VMEM((hp, nb, A_HEAD_DIM), F32),
            pltpu.VMEM((hp, nb, blk), F32),
            pltpu.VMEM((hp, 1, blk), F32),
            pltpu.VMEM((hp, 1, blk), F32),
            pltpu.VMEM((hp, A_HEAD_DIM, blk), F32),
        ],
        compiler_params=_params("parallel", "parallel", "arbitrary"),
        name="moba_attn",
    )(qk, qk, vt, tab)


def _moba_layer(x, w_in, w_out, rel_bias, g, b, batch, seq):
    assert REL_MAX_DIST <= MOBA_BLOCK
    d = D_MODEL
    qk, vt = _moba_qkv(x, w_in[:, :2 * d], w_in[:, 2 * d:].T, batch, seq)
    o = _moba_attention(qk, vt, _moba_bias_tables(rel_bias), batch, seq)
    return _proj_res_ln(o, w_out, x, g, b, "moba_out")


def _pool_kernel(x_ref, halo_ref, win_ref, wgrp_ref, scale_ref, wout_ref, g_ref, b_ref, o_ref,
                 ubuf_ref, ybuf_ref, *, tiles_per_seq):
    tm = x_ref.shape[0]
    hl = POOL_HALO
    ti = pl.program_id(0) % tiles_per_seq
    x = x_ref[...]
    u_halo = _dot(halo_ref[...].astype(BF16), win_ref[...])
    ubuf_ref[0:hl, :] = jnp.where(ti == 0, 0.0, u_halo)
    ubuf_ref[hl:hl + tm, :] = _dot(x.astype(BF16), win_ref[...])
    pos = ti * tm + lax.broadcasted_iota(jnp.int32, (tm, POOL_GROUP), 0)
    for gi, w in enumerate(POOL_WINDOWS):
        lo = gi * POOL_GROUP
        u = ubuf_ref[hl:hl + tm, lo:lo + POOL_GROUP]
        ws = u
        for d in range(1, w):
            ws = ws + ubuf_ref[hl - d:hl - d + tm, lo:lo + POOL_GROUP]
        cnt = jnp.minimum(pos + 1, w).astype(F32)
        pooled = ws / cnt - u
        yg = _dot(pooled.astype(BF16), wgrp_ref[gi]) * scale_ref[:, lo:lo + POOL_GROUP]
        ybuf_ref[:, lo:lo + POOL_GROUP] = yg.astype(BF16)
    y = _dot(ybuf_ref[...], wout_ref[...])
    o_ref[...] = _layer_norm(ALPHA * x + y, g_ref[...], b_ref[...])


def _pool_layer(x, w_in, w_group, scale, w_out, g, b, seq):
    t = x.shape[0]
    tm = ROW_TILE
    hl = POOL_HALO
    assert seq % tm == 0 and tm % hl == 0 and max(POOL_WINDOWS) <= hl
    kern = functools.partial(_pool_kernel, tiles_per_seq=seq // tm)
    ng = len(POOL_WINDOWS)
    return pl.pallas_call(
        kern,
        grid=(t // tm,),
        in_specs=[
            pl.BlockSpec((tm, D_MODEL), lambda i: (i, 0)),
            pl.BlockSpec((hl, D_MODEL), lambda i: (jnp.maximum(i * (tm // hl) - 1, 0), 0)),
            _const_spec((D_MODEL, D_MODEL)),
            _const_spec((ng, POOL_GROUP, POOL_GROUP)),
            _const_spec((1, D_MODEL)),
            _const_spec((D_MODEL, D_MODEL)),
            _const_spec((1, D_MODEL)),
            _const_spec((1, D_MODEL)),
        ],
        out_specs=pl.BlockSpec((tm, D_MODEL), lambda i: (i, 0)),
        out_shape=jax.ShapeDtypeStruct((t, D_MODEL), F32),
        scratch_shapes=[pltpu.VMEM((tm + hl, D_MODEL), F32), pltpu.VMEM((tm, D_MODEL), BF16)],
        compiler_params=_params("parallel"),
        name="pool_layer",
    )(x, x, w_in, w_group, scale, w_out, g, b)


def _log_sigmoid(x):
    return jnp.minimum(x, 0.0) - jnp.log1p(jnp.exp(-jnp.abs(x)))


def _mlstm_kernel(x_ref, qk_ref, v_ref, op_ref, wg_ref, wgt_ref, bg_row_ref, bg_col_ref, cw_ref,
                  ng_ref, wout_ref, g_ref, b_ref, o_ref,
                  c_ref, n_ref, m_ref, cprev_ref, cbuf_ref, hn_ref):
    L = C_CHUNK
    dh = C_HEAD_DIM
    nh = C_HEADS
    hl = CONV_HALO

    @pl.when(pl.program_id(1) == 0)
    def _():
        c_ref[...] = jnp.zeros_like(c_ref)
        n_ref[...] = jnp.zeros_like(n_ref)
        m_ref[...] = jnp.zeros_like(m_ref)
        cprev_ref[...] = jnp.zeros_like(cprev_ref)

    x = x_ref[...]
    xb = x.astype(BF16)
    g_col = _dot(xb, wg_ref[...]) + bg_row_ref[...]
    g_row = _dot_nt(wgt_ref[...], xb) + bg_col_ref[...]

    qk_pre = qk_ref[...]
    cbuf_ref[0:hl, :] = cprev_ref[...]
    cbuf_ref[hl:hl + L, :] = qk_pre
    cprev_ref[...] = qk_pre[L - hl:L, :]
    conv = cw_ref[0:1, :] * cbuf_ref[hl - (C_CONV - 1):hl - (C_CONV - 1) + L, :]
    for j in range(1, C_CONV):
        off = hl - (C_CONV - 1) + j
        conv = conv + cw_ref[j:j + 1, :] * cbuf_ref[off:off + L, :]
    qk = conv * _sigmoid(conv)

    r = lax.broadcasted_iota(jnp.int32, (L, L), 0)
    cc = lax.broadcasted_iota(jnp.int32, (L, L), 1)
    lower = cc <= r

    for hd in range(nh):
        i_col = g_col[:, hd:hd + 1]
        i_row = g_row[hd:hd + 1, :]
        lf_col = _log_sigmoid(g_col[:, nh + hd:nh + hd + 1])
        lf_row = _log_sigmoid(g_row[nh + hd:nh + hd + 1, :])
        b_col = jnp.sum(jnp.where(lower, lf_row, 0.0), axis=1, keepdims=True)
        b_row = jnp.sum(jnp.where(r <= cc, lf_col, 0.0), axis=0, keepdims=True)
        b_last = jnp.sum(lf_row, axis=1, keepdims=True)

        q_h = qk[:, hd * dh:(hd + 1) * dh]
        k_h = qk[:, D_MODEL + hd * dh:D_MODEL + (hd + 1) * dh] * (dh ** -0.5)
        qb = q_h.astype(BF16)
        kb = k_h.astype(BF16)
        vb = v_ref[:, hd * dh:(hd + 1) * dh].astype(BF16)
        c_st = c_ref[hd]
        n_st = n_ref[hd]
        m_prev = m_ref[hd]

        d_intra = jnp.where(lower, b_col - b_row + i_row, NEG)
        m_inter = b_col + m_prev
        m_t = jnp.maximum(m_inter, jnp.max(d_intra, axis=1, keepdims=True))
        w = jnp.exp(d_intra - m_t) * _dot_nt(qb, kb)
        s_inter = jnp.exp(m_inter - m_t)
        num = s_inter * _dot(qb, c_st.astype(BF16)) + _dot(w.astype(BF16), vb)
        den = s_inter * jnp.sum(q_h * n_st, axis=1, keepdims=True) + jnp.sum(w, axis=1, keepdims=True)
        ht = num / jnp.maximum(jnp.abs(den), jnp.exp(-m_t))

        gg_col = b_last - b_col + i_col
        gg_row = b_last - b_row + i_row
        m_new = jnp.maximum(b_last + m_prev, jnp.max(gg_row, axis=1, keepdims=True))
        decay = jnp.exp(b_last + m_prev - m_new)
        kw = k_h * jnp.exp(gg_col - m_new)
        c_ref[hd] = decay * c_st + _dot(jnp.transpose(kw).astype(BF16), vb)
        n_ref[hd] = decay * n_st + jnp.sum(kw, axis=0, keepdims=True)
        m_ref[hd] = m_new

        hc = _sigmoid(op_ref[:, hd * dh:(hd + 1) * dh]) * ht
        mu = jnp.mean(hc, axis=1, keepdims=True)
        hcc = hc - mu
        var = jnp.mean(hcc * hcc, axis=1, keepdims=True)
        hn = hcc * lax.rsqrt(var + LN_EPS) * ng_ref[:, hd * dh:(hd + 1) * dh]
        hn_ref[:, hd * dh:(hd + 1) * dh] = hn.astype(BF16)

    y = _dot(hn_ref[...], wout_ref[...])
    o_ref[...] = _layer_norm(ALPHA * x + y, g_ref[...], b_ref[...])


def _mlstm_layer(x, w_in, b_gates, conv_w, norm_g, w_out, g, b, batch, seq):
    t = x.shape[0]
    L = C_CHUNK
    nc = seq // L
    d = D_MODEL
    nh = C_HEADS
    assert seq % L == 0
    proj = _proj(x, w_in[:, :4 * d], F32, "mlstm_proj")
    w_gate = w_in[:, 4 * d:]
    wg = jnp.pad(w_gate, ((0, 0), (0, GATE_PAD - 2 * nh)))
    wgt = jnp.pad(w_gate.T, ((0, 16 - 2 * nh), (0, 0)))
    bg_row = jnp.pad(b_gates, (0, GATE_PAD - 2 * nh))[None, :].astype(F32)
    bg_col = jnp.pad(b_gates, (0, 16 - 2 * nh))[:, None].astype(F32)
    row = lambda bb, c: (bb * nc + c, 0)
    return pl.pallas_call(
        _mlstm_kernel,
        grid=(batch, nc),
        in_specs=[
            pl.BlockSpec((L, d), row),
            pl.BlockSpec((L, 2 * d), row),
            pl.BlockSpec((L, d), lambda bb, c: (bb * nc + c, 2)),
            pl.BlockSpec((L, d), lambda bb, c: (bb * nc + c, 3)),
            _const_spec((d, GATE_PAD)),
            _const_spec((16, d)),
            _const_spec((1, GATE_PAD)),
            _const_spec((16, 1)),
            _const_spec((C_CONV, 2 * d)),
            _const_spec((1, d)),
            _const_spec((d, d)),
            _const_spec((1, d)),
            _const_spec((1, d)),
        ],
        out_specs=pl.BlockSpec((L, d), row),
        out_shape=jax.ShapeDtypeStruct((t, d), F32),
        scratch_shapes=[
            pltpu.VMEM((nh, C_HEAD_DIM, C_HEAD_DIM), F32),
            pltpu.VMEM((nh, 1, C_HEAD_DIM), F32),
            pltpu.VMEM((nh, 1, 1), F32),
            pltpu.VMEM((CONV_HALO, 2 * d), F32),
            pltpu.VMEM((CONV_HALO + L, 2 * d), F32),
            pltpu.VMEM((L, d), BF16),
        ],
        compiler_params=_params("arbitrary", "arbitrary"),
        name="mlstm_layer",
    )(x, proj, proj, proj, wg.astype(BF16), wgt.astype(BF16), bg_row, bg_col, conv_w.astype(F32),
      norm_g[None, :].astype(F32), w_out, g, b)


def kernel(x, rel_bias, ln_g, ln_b, ffn_w_gu, ffn_w_down, a_w_in, a_w_out, b_w_in, b_w_group, b_scale, b_w_out,
           c_w_in, c_b_gates, c_conv_w, c_norm_g, c_w_out):
    batch, seq, d = x.shape
    h = x.reshape(batch * seq, d)
    bf = lambda w: w.astype(BF16)
    for i in range(DEPTH):
        lg = lambda s: ln_g[i, s][None, :]
        lb = lambda s: ln_b[i, s][None, :]
        h = _ffn(h, bf(ffn_w_gu[i, 0]), bf(ffn_w_down[i, 0]), lg(0), lb(0))
        kind, j = i % N_MIXERS, i // N_MIXERS
        if kind == 0:
            h = _moba_layer(h, bf(a_w_in[j]), bf(a_w_out[j]), rel_bias, lg(1), lb(1), batch, seq)
        elif kind == 1:
            h = _pool_layer(h, bf(b_w_in[j]), bf(b_w_group[j]), b_scale[j][None, :], bf(b_w_out[j]),
                            lg(1), lb(1), seq)
        else:
            h = _mlstm_layer(h, bf(c_w_in[j]), c_b_gates[j], c_conv_w[j], c_norm_g[j], bf(c_w_out[j]),
                             lg(1), lb(1), batch, seq)
        h = _ffn(h, bf(ffn_w_gu[i, 1]), bf(ffn_w_down[i, 1]), lg(2), lb(2))
    return h.reshape(batch, seq, d)
```

```python
import functools
import math

import numpy as np
import jax
import jax.numpy as jnp
from jax import lax
from jax.experimental import pallas as pl
from jax.experimental.pallas import tpu as pltpu

F32 = jnp.float32
BF16 = jnp.bfloat16

D_MODEL = 1024
DEPTH = 4
N_MIXERS = 3
D_FF = 2816
LN_EPS = 1e-5
ALPHA = (2 * DEPTH) ** 0.25
A_HEADS = 8
A_HEAD_DIM = D_MODEL // A_HEADS
MOBA_BLOCK = 256
MOBA_TOPK = 3
REL_BUCKETS = 32
REL_MAX_EXACT = REL_BUCKETS // 2
REL_MAX_DIST = 128
POOL_WINDOWS = (2, 4, 8, 16)
POOL_GROUP = D_MODEL // len(POOL_WINDOWS)
POOL_HALO = 16
C_HEADS = 4
C_HEAD_DIM = D_MODEL // C_HEADS
C_CONV = 4
C_CHUNK = 256
CONV_HALO = 8
GATE_PAD = 128

NEG = -1e30
LOG2E = math.log2(math.e)
MOBA_GROUP = 4
MOBA_HEADS_PER_STEP = 2
MOBA_EXP_ROWS = 64
MOBA_VT_ROWS = 128 + 16
V7X_VMEM_LIMIT = 56 * 1024 * 1024
FFN_CHUNK = 256
ROW_TILE = 512

NT_DIMS = (((1,), (1,)), ((), ()))


def _params(*sem, flags=None):
    return pltpu.CompilerParams(dimension_semantics=sem, vmem_limit_bytes=V7X_VMEM_LIMIT, flags=flags)


def _const_spec(shape):
    nd = len(shape)
    return pl.BlockSpec(shape, lambda *_: (0,) * nd, pipeline_mode=pl.Buffered(1))


def _layer_norm(z, g, b):
    mu = jnp.mean(z, axis=-1, keepdims=True)
    zc = z - mu
    var = jnp.mean(zc * zc, axis=-1, keepdims=True)
    return zc * lax.rsqrt(var + LN_EPS) * g + b


def _sigmoid(x):
    return 1.0 / (1.0 + jnp.exp(-x))


def _dot(a, b):
    return jnp.dot(a, b, preferred_element_type=F32)


def _dot_nt(a, b):
    return lax.dot_general(a, b, NT_DIMS, preferred_element_type=F32)


def _ffn_kernel(x_ref, wgu_ref, wd_ref, g_ref, b_ref, o_ref, xb_ref, h_ref):
    x = x_ref[...]
    xb_ref[...] = x.astype(BF16)
    for c in range(D_FF // FFN_CHUNK):
        lo = c * FFN_CHUNK
        xb = xb_ref[...]
        gate = _dot(xb, wgu_ref[:, lo:lo + FFN_CHUNK])
        up = _dot(xb, wgu_ref[:, D_FF + lo:D_FF + lo + FFN_CHUNK])
        h_ref[:, lo:lo + FFN_CHUNK] = (gate * _sigmoid(gate) * up).astype(BF16)
    y = _dot(h_ref[...], wd_ref[...])
    o_ref[...] = _layer_norm(ALPHA * x + 0.5 * y, g_ref[...], b_ref[...])


def _ffn(x, w_gu, w_down, g, b):
    t = x.shape[0]
    tm = ROW_TILE
    return pl.pallas_call(
        _ffn_kernel,
        grid=(t // tm,),
        in_specs=[
            pl.BlockSpec((tm, D_MODEL), lambda i: (i, 0)),
            _const_spec((D_MODEL, 2 * D_FF)),
            _const_spec((D_FF, D_MODEL)),
            _const_spec((1, D_MODEL)),
            _const_spec((1, D_MODEL)),
        ],
        out_specs=pl.BlockSpec((tm, D_MODEL), lambda i: (i, 0)),
        out_shape=jax.ShapeDtypeStruct((t, D_MODEL), F32),
        scratch_shapes=[pltpu.VMEM((tm, D_MODEL), BF16), pltpu.VMEM((tm, D_FF), BF16)],
        compiler_params=_params("parallel"),
        name="ffn",
    )(x, w_gu, w_down, g, b)


def _proj_kernel(x_ref, w_ref, o_ref):
    o_ref[...] = _dot(x_ref[...].astype(BF16), w_ref[...]).astype(o_ref.dtype)


def _proj(x, w, out_dtype, name):
    t, k = x.shape
    n = w.shape[1]
    tm = ROW_TILE
    return pl.pallas_call(
        _proj_kernel,
        grid=(t // tm,),
        in_specs=[pl.BlockSpec((tm, k), lambda i: (i, 0)), _const_spec((k, n))],
        out_specs=pl.BlockSpec((tm, n), lambda i: (i, 0)),
        out_shape=jax.ShapeDtypeStruct((t, n), out_dtype),
        compiler_params=_params("parallel"),
        name=name,
    )(x, w)


def _proj_res_ln_kernel(a_ref, w_ref, x_ref, g_ref, b_ref, o_ref):
    y = _dot(a_ref[...].astype(BF16), w_ref[...])
    o_ref[...] = _layer_norm(ALPHA * x_ref[...] + y, g_ref[...], b_ref[...])


def _proj_res_ln(a, w, x, g, b, name):
    t = x.shape[0]
    tm = ROW_TILE
    row = lambda i: (i, 0)
    return pl.pallas_call(
        _proj_res_ln_kernel,
        grid=(t // tm,),
        in_specs=[
            pl.BlockSpec((tm, D_MODEL), row),
            _const_spec((D_MODEL, D_MODEL)),
            pl.BlockSpec((tm, D_MODEL), row),
            _const_spec((1, D_MODEL)),
            _const_spec((1, D_MODEL)),
        ],
        out_specs=pl.BlockSpec((tm, D_MODEL), row),
        out_shape=jax.ShapeDtypeStruct((t, D_MODEL), F32),
        compiler_params=_params("parallel"),
        name=name,
    )(a, w, x, g, b)


def _t5_bucket(dist):
    n = jnp.maximum(dist, 0)
    is_small = n < REL_MAX_EXACT
    nf = jnp.maximum(n, 1).astype(F32)
    large = REL_MAX_EXACT + (jnp.log(nf / REL_MAX_EXACT) / math.log(REL_MAX_DIST / REL_MAX_EXACT)
                             * (REL_BUCKETS - REL_MAX_EXACT)).astype(jnp.int32)
    large = jnp.minimum(large, REL_BUCKETS - 1)
    return jnp.where(is_small, n, large)


def _moba_bias_tables(rel_bias):
    blk = MOBA_BLOCK
    dist = jnp.arange(blk)[None, :] - jnp.arange(blk)[:, None]

    def lookup(bucket):
        onehot = bucket[None, :, :, None] == jnp.arange(REL_BUCKETS)
        return jnp.sum(jnp.where(onehot, rel_bias.T[:, None, None, :], 0.0), axis=-1)

    far = rel_bias[REL_BUCKETS - 1][:, None, None]
    own = jnp.where(dist[None] >= 0, (lookup(_t5_bucket(dist)) - far) * LOG2E, NEG)
    adj = (lookup(_t5_bucket(dist + blk)) - far) * LOG2E
    return jnp.stack([own, adj, jnp.zeros_like(adj)], axis=1).astype(F32)


def _moba_qkv_kernel(x_ref, wqk_ref, wvt_ref, qk_ref, vt_ref):
    xb = x_ref[...].astype(BF16)
    qk_ref[...] = _dot(xb, wqk_ref[...]).astype(BF16)
    vt = _dot_nt(wvt_ref[...], xb).astype(BF16)
    dh, rows = A_HEAD_DIM, MOBA_VT_ROWS
    for h in range(A_HEADS):
        vt_ref[h * rows:h * rows + dh, :] = vt[h * dh:(h + 1) * dh, :]
        vt_ref[h * rows + dh:(h + 1) * rows, :] = jnp.ones((rows - dh, vt.shape[1]), BF16)


def _moba_qkv(x, w_qk, w_vt, batch, seq):
    t = x.shape[0]
    tm = ROW_TILE
    tps = seq // tm
    return pl.pallas_call(
        _moba_qkv_kernel,
        grid=(t // tm,),
        in_specs=[
            pl.BlockSpec((tm, D_MODEL), lambda i: (i, 0)),
            _const_spec((D_MODEL, 2 * D_MODEL)),
            _const_spec((D_MODEL, D_MODEL)),
        ],
        out_specs=[
            pl.BlockSpec((tm, 2 * D_MODEL), lambda i: (i, 0)),
            pl.BlockSpec((None, A_HEADS * MOBA_VT_ROWS, tm), lambda i: (i // tps, 0, i % tps)),
        ],
        out_shape=[
            jax.ShapeDtypeStruct((t, 2 * D_MODEL), BF16),
            jax.ShapeDtypeStruct((batch, A_HEADS * MOBA_VT_ROWS, seq), BF16),
        ],
        compiler_params=_params("parallel"),
        name="moba_qkv",
    )(x, w_qk, w_vt)


def _moba_kernel(q_ref, k_ref, vt_ref, tab_ref, o_ref, kmean_ref, sel_ref, t_ref, p_ref, *, nb):
    blk = MOBA_BLOCK
    dh = A_HEAD_DIM
    i = pl.program_id(2)

    @pl.when(i == 0)
    def _():
        for hh in range(MOBA_HEADS_PER_STEP):
            for j in range(nb):
                kb = k_ref[j * blk:(j + 1) * blk, hh * dh:(hh + 1) * dh].astype(F32)
                kmean_ref[hh, j:j + 1, :] = jnp.mean(kb, axis=0, keepdims=True)

    jidx = lax.broadcasted_iota(jnp.int32, (nb, blk), 0)
    for hh in range(MOBA_HEADS_PER_STEP):
        q = q_ref[:, hh * dh:(hh + 1) * dh]
        gate = _dot_nt(kmean_ref[hh].astype(BF16), q)
        cnt = jnp.zeros((nb, blk), F32)
        for jp in range(nb - 1):
            row = gate[jp:jp + 1, :]
            beats = (row > gate) | ((row == gate) & (jp < jidx))
            cnt = cnt + jnp.where(beats & (jp < i), 1.0, 0.0)
        chosen = ((cnt < MOBA_TOPK) & (jidx < i)) | (jidx == i)
        sel_ref[hh] = jnp.where(chosen, 1.0, 0.0)

    gb = MOBA_GROUP

    def attend(ngroups):
        nblocks = ngroups * gb
        nk = nblocks * blk
        first_dynamic = (ngroups - 1) * gb - 1
        heads = range(MOBA_HEADS_PER_STEP)

        m8 = [None] * len(heads)
        for j in range(nblocks):
            for hh in heads:
                t = _dot_nt(k_ref[j * blk:(j + 1) * blk, hh * dh:(hh + 1) * dh], q_ref[:, hh * dh:(hh + 1) * dh])
                if j >= first_dynamic:
                    t = t + tab_ref[hh, jnp.clip(i - j, 0, 2)]
                t = jnp.where(sel_ref[hh, j:j + 1, :] > 0.5, t, NEG)
                t_ref[hh, j * blk:(j + 1) * blk, :] = t
                mb = jnp.max(t.reshape(blk // 8, 8, blk), axis=0)
                m8[hh] = mb if j == 0 else jnp.maximum(m8[hh], mb)
        m = [jnp.max(m8[hh], axis=0, keepdims=True) for hh in heads]
        sub = MOBA_EXP_ROWS
        for j in range(nblocks):
            for hh in heads:
                for r in range(j * blk, (j + 1) * blk, sub):
                    p_ref[hh, r:r + sub, :] = jnp.exp2(t_ref[hh, r:r + sub, :] - m[hh]).astype(BF16)
        rows = MOBA_VT_ROWS
        for hh in heads:
            o_aug = _dot(vt_ref[hh * rows:(hh + 1) * rows, 0:nk], p_ref[hh, 0:nk, :])
            o_t = o_aug[0:dh, :] / o_aug[dh:dh + 1, :]
            o_ref[:, hh * dh:(hh + 1) * dh] = jnp.transpose(o_t).astype(o_ref.dtype)

    for ngroups in range(1, nb // gb + 1):
        pl.when(i // gb + 1 == ngroups)(functools.partial(attend, ngroups))


def _moba_attention(qk, vt, tab, batch, seq):
    blk = MOBA_BLOCK
    nb = seq // blk
    hp = MOBA_HEADS_PER_STEP
    w = hp * A_HEAD_DIM
    ngrp = A_HEADS // hp
    assert seq % blk == 0 and nb % MOBA_GROUP == 0 and A_HEADS % hp == 0
    kern = functools.partial(_moba_kernel, nb=nb)
    return pl.pallas_call(
        kern,
        grid=(batch, ngrp, nb),
        in_specs=[
            pl.BlockSpec((blk, w), lambda b, h, i: (b * nb + i, h)),
            pl.BlockSpec((seq, w), lambda b, h, i: (b, ngrp + h)),
            pl.BlockSpec((None, hp * MOBA_VT_ROWS, seq), lambda b, h, i: (b, h, 0)),
            pl.BlockSpec((hp, 3, blk, blk), lambda b, h, i: (h, 0, 0, 0)),
        ],
        out_specs=pl.BlockSpec((blk, w), lambda b, h, i: (b * nb + i, h)),
        out_shape=jax.ShapeDtypeStruct((batch * seq, D_MODEL), BF16),
        scratch_shapes=[
            pltpu.VMEM((hp, nb, A_HEAD_DIM), F32),
            pltpu.VMEM((hp, nb, blk), F32),
            pltpu.VMEM((hp, seq, blk), F32),
            pltpu.VMEM((hp, seq, blk), BF16),
        ],
        compiler_params=_params("parallel", "parallel", "arbitrary"),
        name="moba_attn",
    )(qk, qk, vt, tab)


def _moba_layer(x, w_in, w_out, rel_bias, g, b, batch, seq):
    assert REL_MAX_DIST <= MOBA_BLOCK
    d = D_MODEL
    c1 = (A_HEAD_DIM ** -0.5) * LOG2E
    w_qk = jnp.concatenate([w_in[:, :d] * c1, w_in[:, d:2 * d]], axis=1).astype(BF16)
    w_vt = w_in[:, 2 * d:].T.astype(BF16)
    qk, vt = _moba_qkv(x, w_qk, w_vt, batch, seq)
    o = _moba_attention(qk, vt, _moba_bias_tables(rel_bias), batch, seq)
    return _proj_res_ln(o, w_out, x, g, b, "moba_out")


def _pool_kernel(x_ref, halo_ref, win_ref, wgrp_ref, scale_ref, wout_ref, g_ref, b_ref, o_ref,
                 ubuf_ref, ybuf_ref, *, tiles_per_seq):
    tm = x_ref.shape[0]
    hl = POOL_HALO
    ti = pl.program_id(0) % tiles_per_seq
    x = x_ref[...]
    u_halo = _dot(halo_ref[...].astype(BF16), win_ref[...])
    ubuf_ref[0:hl, :] = jnp.where(ti == 0, 0.0, u_halo)
    ubuf_ref[hl:hl + tm, :] = _dot(x.astype(BF16), win_ref[...])
    pos = ti * tm + lax.broadcasted_iota(jnp.int32, (tm, POOL_GROUP), 0)
    for gi, w in enumerate(POOL_WINDOWS):
        lo = gi * POOL_GROUP
        u = ubuf_ref[hl:hl + tm, lo:lo + POOL_GROUP]
        ws = u
        for d in range(1, w):
            ws = ws + ubuf_ref[hl - d:hl - d + tm, lo:lo + POOL_GROUP]
        cnt = jnp.minimum(pos + 1, w).astype(F32)
        pooled = ws / cnt - u
        yg = _dot(pooled.astype(BF16), wgrp_ref[gi]) * scale_ref[:, lo:lo + POOL_GROUP]
        ybuf_ref[:, lo:lo + POOL_GROUP] = yg.astype(BF16)
    y = _dot(ybuf_ref[...], wout_ref[...])
    o_ref[...] = _layer_norm(ALPHA * x + y, g_ref[...], b_ref[...])


def _pool_layer(x, w_in, w_group, scale, w_out, g, b, seq):
    t = x.shape[0]
    tm = ROW_TILE
    hl = POOL_HALO
    assert seq % tm == 0 and tm % hl == 0 and max(POOL_WINDOWS) <= hl
    kern = functools.partial(_pool_kernel, tiles_per_seq=seq // tm)
    ng = len(POOL_WINDOWS)
    return pl.pallas_call(
        kern,
        grid=(t // tm,),
        in_specs=[
            pl.BlockSpec((tm, D_MODEL), lambda i: (i, 0)),
            pl.BlockSpec((hl, D_MODEL), lambda i: (jnp.maximum(i * (tm // hl) - 1, 0), 0)),
            _const_spec((D_MODEL, D_MODEL)),
            _const_spec((ng, POOL_GROUP, POOL_GROUP)),
            _const_spec((1, D_MODEL)),
            _const_spec((D_MODEL, D_MODEL)),
            _const_spec((1, D_MODEL)),
            _const_spec((1, D_MODEL)),
        ],
        out_specs=pl.BlockSpec((tm, D_MODEL), lambda i: (i, 0)),
        out_shape=jax.ShapeDtypeStruct((t, D_MODEL), F32),
        scratch_shapes=[pltpu.VMEM((tm + hl, D_MODEL), F32), pltpu.VMEM((tm, D_MODEL), BF16)],
        compiler_params=_params("parallel"),
        name="pool_layer",
    )(x, x, w_in, w_group, scale, w_out, g, b)


def _log_sigmoid(x):
    return jnp.minimum(x, 0.0) - jnp.log1p(jnp.exp(-jnp.abs(x)))


def _mlstm_kernel(x_ref, qk_ref, v_ref, op_ref, wg_ref, wgt_ref, bg_row_ref, bg_col_ref, cw_ref,
                  ng_ref, wout_ref, g_ref, b_ref, o_ref,
                  c_ref, n_ref, m_ref, cprev_ref, cbuf_ref, hn_ref):
    L = C_CHUNK
    dh = C_HEAD_DIM
    nh = C_HEADS
    hl = CONV_HALO

    @pl.when(pl.program_id(1) == 0)
    def _():
        c_ref[...] = jnp.zeros_like(c_ref)
        n_ref[...] = jnp.zeros_like(n_ref)
        m_ref[...] = jnp.zeros_like(m_ref)
        cprev_ref[...] = jnp.zeros_like(cprev_ref)

    x = x_ref[...]
    xb = x.astype(BF16)
    g_col = _dot(xb, wg_ref[...]) + bg_row_ref[...]
    g_row = _dot_nt(wgt_ref[...], xb) + bg_col_ref[...]

    qk_pre = qk_ref[...]
    cbuf_ref[0:hl, :] = cprev_ref[...]
    cbuf_ref[hl:hl + L, :] = qk_pre
    cprev_ref[...] = qk_pre[L - hl:L, :]
    conv = cw_ref[0:1, :] * cbuf_ref[hl - (C_CONV - 1):hl - (C_CONV - 1) + L, :]
    for j in range(1, C_CONV):
        off = hl - (C_CONV - 1) + j
        conv = conv + cw_ref[j:j + 1, :] * cbuf_ref[off:off + L, :]
    qk = conv * _sigmoid(conv)

    r = lax.broadcasted_iota(jnp.int32, (L, L), 0)
    cc = lax.broadcasted_iota(jnp.int32, (L, L), 1)
    lower = cc <= r

    for hd in range(nh):
        i_col = g_col[:, hd:hd + 1]
        i_row = g_row[hd:hd + 1, :]
        lf_col = _log_sigmoid(g_col[:, nh + hd:nh + hd + 1])
        lf_row = _log_sigmoid(g_row[nh + hd:nh + hd + 1, :])
        b_col = jnp.sum(jnp.where(lower, lf_row, 0.0), axis=1, keepdims=True)
        b_row = jnp.sum(jnp.where(r <= cc, lf_col, 0.0), axis=0, keepdims=True)
        b_last = jnp.sum(lf_row, axis=1, keepdims=True)

        q_h = qk[:, hd * dh:(hd + 1) * dh]
        k_h = qk[:, D_MODEL + hd * dh:D_MODEL + (hd + 1) * dh] * (dh ** -0.5)
        qb = q_h.astype(BF16)
        kb = k_h.astype(BF16)
        vb = v_ref[:, hd * dh:(hd + 1) * dh].astype(BF16)
        c_st = c_ref[hd]
        n_st = n_ref[hd]
        m_prev = m_ref[hd]

        d_intra = jnp.where(lower, b_col - b_row + i_row, NEG)
        m_inter = b_col + m_prev
        m_t = jnp.maximum(m_inter, jnp.max(d_intra, axis=1, keepdims=True))
        w = jnp.exp(d_intra - m_t) * _dot_nt(qb, kb)
        s_inter = jnp.exp(m_inter - m_t)
        num = s_inter * _dot(qb, c_st.astype(BF16)) + _dot(w.astype(BF16), vb)
        den = s_inter * jnp.sum(q_h * n_st, axis=1, keepdims=True) + jnp.sum(w, axis=1, keepdims=True)
        ht = num / jnp.maximum(jnp.abs(den), jnp.exp(-m_t))

        gg_col = b_last - b_col + i_col
        gg_row = b_last - b_row + i_row
        m_new = jnp.maximum(b_last + m_prev, jnp.max(gg_row, axis=1, keepdims=True))
        decay = jnp.exp(b_last + m_prev - m_new)
        kw = k_h * jnp.exp(gg_col - m_new)
        c_ref[hd] = decay * c_st + _dot(jnp.transpose(kw).astype(BF16), vb)
        n_ref[hd] = decay * n_st + jnp.sum(kw, axis=0, keepdims=True)
        m_ref[hd] = m_new

        hc = _sigmoid(op_ref[:, hd * dh:(hd + 1) * dh]) * ht
        mu = jnp.mean(hc, axis=1, keepdims=True)
        hcc = hc - mu
        var = jnp.mean(hcc * hcc, axis=1, keepdims=True)
        hn = hcc * lax.rsqrt(var + LN_EPS) * ng_ref[:, hd * dh:(hd + 1) * dh]
        hn_ref[:, hd * dh:(hd + 1) * dh] = hn.astype(BF16)

    y = _dot(hn_ref[...], wout_ref[...])
    o_ref[...] = _layer_norm(ALPHA * x + y, g_ref[...], b_ref[...])


def _mlstm_layer(x, w_in, b_gates, conv_w, norm_g, w_out, g, b, batch, seq):
    t = x.shape[0]
    L = C_CHUNK
    nc = seq // L
    d = D_MODEL
    nh = C_HEADS
    assert seq % L == 0
    proj = _proj(x, w_in[:, :4 * d], F32, "mlstm_proj")
    w_gate = w_in[:, 4 * d:]
    wg = jnp.pad(w_gate, ((0, 0), (0, GATE_PAD - 2 * nh)))
    wgt = jnp.pad(w_gate.T, ((0, 16 - 2 * nh), (0, 0)))
    bg_row = jnp.pad(b_gates, (0, GATE_PAD - 2 * nh))[None, :].astype(F32)
    bg_col = jnp.pad(b_gates, (0, 16 - 2 * nh))[:, None].astype(F32)
    row = lambda bb, c: (bb * nc + c, 0)
    return pl.pallas_call(
        _mlstm_kernel,
        grid=(batch, nc),
        in_specs=[
            pl.BlockSpec((L, d), row),
            pl.BlockSpec((L, 2 * d), row),
            pl.BlockSpec((L, d), lambda bb, c: (bb * nc + c, 2)),
            pl.BlockSpec((L, d), lambda bb, c: (bb * nc + c, 3)),
            _const_spec((d, GATE_PAD)),
            _const_spec((16, d)),
            _const_spec((1, GATE_PAD)),
            _const_spec((16, 1)),
            _const_spec((C_CONV, 2 * d)),
            _const_spec((1, d)),
            _const_spec((d, d)),
            _const_spec((1, d)),
            _const_spec((1, d)),
        ],
        out_specs=pl.BlockSpec((L, d), row),
        out_shape=jax.ShapeDtypeStruct((t, d), F32),
        scratch_shapes=[
            pltpu.VMEM((nh, C_HEAD_DIM, C_HEAD_DIM), F32),
            pltpu.VMEM((nh, 1, C_HEAD_DIM), F32),
            pltpu.VMEM((nh, 1, 1), F32),
            pltpu.VMEM((CONV_HALO, 2 * d), F32),
            pltpu.VMEM((CONV_HALO + L, 2 * d), F32),
            pltpu.VMEM((L, d), BF16),
        ],
        compiler_params=_params("arbitrary", "arbitrary"),
        name="mlstm_layer",
    )(x, proj, proj, proj, wg.astype(BF16), wgt.astype(BF16), bg_row, bg_col, conv_w.astype(F32),
      norm_g[None, :].astype(F32), w_out, g, b)


def kernel(x, rel_bias, ln_g, ln_b, ffn_w_gu, ffn_w_down, a_w_in, a_w_out, b_w_in, b_w_group, b_scale, b_w_out,
           c_w_in, c_b_gates, c_conv_w, c_norm_g, c_w_out):
    batch, seq, d = x.shape
    h = x.reshape(batch * seq, d)
    bf = lambda w: w.astype(BF16)
    for i in range(DEPTH):
        lg = lambda s: ln_g[i, s][None, :]
        lb = lambda s: ln_b[i, s][None, :]
        h = _ffn(h, bf(ffn_w_gu[i, 0]), bf(ffn_w_down[i, 0]), lg(0), lb(0))
        kind, j = i % N_MIXERS, i // N_MIXERS
        if kind == 0:
            h = _moba_layer(h, a_w_in[j], bf(a_w_out[j]), rel_bias, lg(1), lb(1), batch, seq)
        elif kind == 1:
            h = _pool_layer(h, bf(b_w_in[j]), bf(b_w_group[j]), b_scale[j][None, :], bf(b_w_out[j]),
                            lg(1), lb(1), seq)
        else:
            h = _mlstm_layer(h, bf(c_w_in[j]), c_b_gates[j], c_conv_w[j], c_norm_g[j], bf(c_w_out[j]),
                             lg(1), lb(1), batch, seq)
        h = _ffn(h, bf(ffn_w_gu[i, 1]), bf(ffn_w_down[i, 1]), lg(2), lb(2))
    return h.reshape(batch, seq, d)
```

```python
import functools
import math

import numpy as np
import jax
import jax.numpy as jnp
from jax import lax
from jax.experimental import pallas as pl
from jax.experimental.pallas import tpu as pltpu

F32 = jnp.float32
BF16 = jnp.bfloat16

D_MODEL = 1024
DEPTH = 4
N_MIXERS = 3
D_FF = 2816
LN_EPS = 1e-5
ALPHA = (2 * DEPTH) ** 0.25
A_HEADS = 8
A_HEAD_DIM = D_MODEL // A_HEADS
MOBA_BLOCK = 256
MOBA_TOPK = 3
REL_BUCKETS = 32
REL_MAX_EXACT = REL_BUCKETS // 2
REL_MAX_DIST = 128
POOL_WINDOWS = (2, 4, 8, 16)
POOL_GROUP = D_MODEL // len(POOL_WINDOWS)
POOL_HALO = 16
C_HEADS = 4
C_HEAD_DIM = D_MODEL // C_HEADS
C_CONV = 4
C_CHUNK = 256
CONV_HALO = 8
GATE_PAD = 128

NEG = -1e30
LOG2E = math.log2(math.e)
MOBA_GROUP = 4
MOBA_HEADS_PER_STEP = 2
MOBA_EXP_ROWS = 64
MOBA_VT_ROWS = 128 + 16
V7X_VMEM_LIMIT = 56 * 1024 * 1024
FFN_CHUNK = 256
FFN_ROW_TILE = 1024
FFN_SUBTILES = 4
ROW_TILE = 512

NT_DIMS = (((1,), (1,)), ((), ()))


def _params(*sem, flags=None):
    return pltpu.CompilerParams(dimension_semantics=sem, vmem_limit_bytes=V7X_VMEM_LIMIT, flags=flags)


def _const_spec(shape):
    nd = len(shape)
    return pl.BlockSpec(shape, lambda *_: (0,) * nd, pipeline_mode=pl.Buffered(1))


def _layer_norm(z, g, b):
    mu = jnp.mean(z, axis=-1, keepdims=True)
    zc = z - mu
    var = jnp.mean(zc * zc, axis=-1, keepdims=True)
    return zc * lax.rsqrt(var + LN_EPS) * g + b


def _sigmoid(x):
    return 1.0 / (1.0 + jnp.exp(-x))


def _dot(a, b):
    return jnp.dot(a, b, preferred_element_type=F32)


def _dot_nt(a, b):
    return lax.dot_general(a, b, NT_DIMS, preferred_element_type=F32)


def _ffn_kernel(x_ref, wgu_ref, wd_ref, g_ref, b_ref, o_ref, xb_ref, h_ref):
    sm = x_ref.shape[0] // FFN_SUBTILES
    for s in range(FFN_SUBTILES):
        rows = slice(s * sm, (s + 1) * sm)
        x = x_ref[rows, :]
        xb_ref[rows, :] = x.astype(BF16)
        for c in range(D_FF // FFN_CHUNK):
            lo = c * FFN_CHUNK
            xb = xb_ref[rows, :]
            gate = _dot(xb, wgu_ref[:, lo:lo + FFN_CHUNK])
            up = _dot(xb, wgu_ref[:, D_FF + lo:D_FF + lo + FFN_CHUNK])
            h_ref[rows, lo:lo + FFN_CHUNK] = (gate * _sigmoid(gate) * up).astype(BF16)
        y = _dot(h_ref[rows, :], wd_ref[...])
        o_ref[rows, :] = _layer_norm(ALPHA * x + 0.5 * y, g_ref[...], b_ref[...])


def _ffn(x, w_gu, w_down, g, b):
    t = x.shape[0]
    tm = FFN_ROW_TILE
    return pl.pallas_call(
        _ffn_kernel,
        grid=(t // tm,),
        in_specs=[
            pl.BlockSpec((tm, D_MODEL), lambda i: (i, 0)),
            _const_spec((D_MODEL, 2 * D_FF)),
            _const_spec((D_FF, D_MODEL)),
            _const_spec((1, D_MODEL)),
            _const_spec((1, D_MODEL)),
        ],
        out_specs=pl.BlockSpec((tm, D_MODEL), lambda i: (i, 0)),
        out_shape=jax.ShapeDtypeStruct((t, D_MODEL), F32),
        scratch_shapes=[pltpu.VMEM((tm, D_MODEL), BF16), pltpu.VMEM((tm, D_FF), BF16)],
        compiler_params=_params("parallel"),
        name="ffn",
    )(x, w_gu, w_down, g, b)


def _proj_kernel(x_ref, w_ref, o_ref):
    o_ref[...] = _dot(x_ref[...].astype(BF16), w_ref[...]).astype(o_ref.dtype)


def _proj(x, w, out_dtype, name):
    t, k = x.shape
    n = w.shape[1]
    tm = ROW_TILE
    return pl.pallas_call(
        _proj_kernel,
        grid=(t // tm,),
        in_specs=[pl.BlockSpec((tm, k), lambda i: (i, 0)), _const_spec((k, n))],
        out_specs=pl.BlockSpec((tm, n), lambda i: (i, 0)),
        out_shape=jax.ShapeDtypeStruct((t, n), out_dtype),
        compiler_params=_params("parallel"),
        name=name,
    )(x, w)


def _proj_res_ln_kernel(a_ref, w_ref, x_ref, g_ref, b_ref, o_ref):
    y = _dot(a_ref[...].astype(BF16), w_ref[...])
    o_ref[...] = _layer_norm(ALPHA * x_ref[...] + y, g_ref[...], b_ref[...])


def _proj_res_ln(a, w, x, g, b, name):
    t = x.shape[0]
    tm = ROW_TILE
    row = lambda i: (i, 0)
    return pl.pallas_call(
        _proj_res_ln_kernel,
        grid=(t // tm,),
        in_specs=[
            pl.BlockSpec((tm, D_MODEL), row),
            _const_spec((D_MODEL, D_MODEL)),
            pl.BlockSpec((tm, D_MODEL), row),
            _const_spec((1, D_MODEL)),
            _const_spec((1, D_MODEL)),
        ],
        out_specs=pl.BlockSpec((tm, D_MODEL), row),
        out_shape=jax.ShapeDtypeStruct((t, D_MODEL), F32),
        compiler_params=_params("parallel"),
        name=name,
    )(a, w, x, g, b)


def _t5_bucket(dist):
    n = jnp.maximum(dist, 0)
    is_small = n < REL_MAX_EXACT
    nf = jnp.maximum(n, 1).astype(F32)
    large = REL_MAX_EXACT + (jnp.log(nf / REL_MAX_EXACT) / math.log(REL_MAX_DIST / REL_MAX_EXACT)
                             * (REL_BUCKETS - REL_MAX_EXACT)).astype(jnp.int32)
    large = jnp.minimum(large, REL_BUCKETS - 1)
    return jnp.where(is_small, n, large)


def _moba_bias_tables(rel_bias):
    blk = MOBA_BLOCK
    dist = jnp.arange(blk)[None, :] - jnp.arange(blk)[:, None]

    def lookup(bucket):
        onehot = bucket[None, :, :, None] == jnp.arange(REL_BUCKETS)
        return jnp.sum(jnp.where(onehot, rel_bias.T[:, None, None, :], 0.0), axis=-1)

    far = rel_bias[REL_BUCKETS - 1][:, None, None]
    own = jnp.where(dist[None] >= 0, (lookup(_t5_bucket(dist)) - far) * LOG2E, NEG)
    adj = (lookup(_t5_bucket(dist + blk)) - far) * LOG2E
    return jnp.stack([own, adj, jnp.zeros_like(adj)], axis=1).astype(F32)


def _moba_qkv_kernel(x_ref, wqk_ref, wvt_ref, qk_ref, vt_ref):
    xb = x_ref[...].astype(BF16)
    qk_ref[...] = _dot(xb, wqk_ref[...]).astype(BF16)
    vt = _dot_nt(wvt_ref[...], xb).astype(BF16)
    dh, rows = A_HEAD_DIM, MOBA_VT_ROWS
    for h in range(A_HEADS):
        vt_ref[h * rows:h * rows + dh, :] = vt[h * dh:(h + 1) * dh, :]
        vt_ref[h * rows + dh:(h + 1) * rows, :] = jnp.ones((rows - dh, vt.shape[1]), BF16)


def _moba_qkv(x, w_qk, w_vt, batch, seq):
    t = x.shape[0]
    tm = ROW_TILE
    tps = seq // tm
    return pl.pallas_call(
        _moba_qkv_kernel,
        grid=(t // tm,),
        in_specs=[
            pl.BlockSpec((tm, D_MODEL), lambda i: (i, 0)),
            _const_spec((D_MODEL, 2 * D_MODEL)),
            _const_spec((D_MODEL, D_MODEL)),
        ],
        out_specs=[
            pl.BlockSpec((tm, 2 * D_MODEL), lambda i: (i, 0)),
            pl.BlockSpec((None, A_HEADS * MOBA_VT_ROWS, tm), lambda i: (i // tps, 0, i % tps)),
        ],
        out_shape=[
            jax.ShapeDtypeStruct((t, 2 * D_MODEL), BF16),
            jax.ShapeDtypeStruct((batch, A_HEADS * MOBA_VT_ROWS, seq), BF16),
        ],
        compiler_params=_params("parallel"),
        name="moba_qkv",
    )(x, w_qk, w_vt)


def _moba_kernel(q_ref, k_ref, vt_ref, tab_ref, o_ref, kmean_ref, sel_ref, t_ref, p_ref, m_ref, *, nb):
    blk = MOBA_BLOCK
    dh = A_HEAD_DIM
    gb = MOBA_GROUP
    rows = MOBA_VT_ROWS
    sub = MOBA_EXP_ROWS
    heads = range(MOBA_HEADS_PER_STEP)
    s = pl.program_id(2)

    @pl.when(s == 0)
    def _():
        for hh in heads:
            for j in range(nb):
                kb = k_ref[j * blk:(j + 1) * blk, hh * dh:(hh + 1) * dh].astype(F32)
                kmean_ref[hh, j:j + 1, :] = jnp.mean(kb, axis=0, keepdims=True)
        o_ref[...] = jnp.zeros_like(o_ref)

    def select():
        jidx = lax.broadcasted_iota(jnp.int32, (nb, blk), 0)
        for hh in heads:
            q = q_ref[:, hh * dh:(hh + 1) * dh]
            gate = _dot_nt(kmean_ref[hh].astype(BF16), q)
            cnt = jnp.zeros((nb, blk), F32)
            for jp in range(nb - 1):
                row = gate[jp:jp + 1, :]
                beats = (row > gate) | ((row == gate) & (jp < jidx))
                cnt = cnt + jnp.where(beats & (jp < s), 1.0, 0.0)
            chosen = ((cnt < MOBA_TOPK) & (jidx < s)) | (jidx == s)
            sel_ref[hh] = jnp.where(chosen, 1.0, 0.0)

    def step(ng_score, ng_finish):
        first_dynamic = (ng_score - 1) * gb - 1
        nk_finish = ng_finish * gb * blk
        if ng_finish:
            m_prev = [m_ref[hh] for hh in heads]
        if ng_score:
            select()
        m8 = [None] * len(heads)
        for j in range(max(ng_score, ng_finish) * gb):
            if j < ng_finish * gb:
                for hh in heads:
                    for r in range(j * blk, (j + 1) * blk, sub):
                        p_ref[hh, r:r + sub, :] = jnp.exp2(t_ref[hh, r:r + sub, :] - m_prev[hh]).astype(BF16)
            if j < ng_score * gb:
                for hh in heads:
                    t = _dot_nt(k_ref[j * blk:(j + 1) * blk, hh * dh:(hh + 1) * dh],
                                q_ref[:, hh * dh:(hh + 1) * dh])
                    if j >= first_dynamic:
                        t = t + tab_ref[hh, jnp.clip(s - j, 0, 2)]
                    t = jnp.where(sel_ref[hh, j:j + 1, :] > 0.5, t, NEG)
                    t_ref[hh, j * blk:(j + 1) * blk, :] = t
                    mb = jnp.max(t.reshape(blk // 8, 8, blk), axis=0)
                    m8[hh] = mb if j == 0 else jnp.maximum(m8[hh], mb)
        if ng_score:
            for hh in heads:
                m_ref[hh] = jnp.max(m8[hh], axis=0, keepdims=True)
        if ng_finish:
            for hh in heads:
                o_aug = _dot(vt_ref[hh * rows:(hh + 1) * rows, 0:nk_finish], p_ref[hh, 0:nk_finish, :])
                o_t = o_aug[0:dh, :] / o_aug[dh:dh + 1, :]
                o_ref[:, hh * dh:(hh + 1) * dh] = jnp.transpose(o_t).astype(o_ref.dtype)

    ng_score = jnp.where(s < nb, s // gb + 1, 0)
    ng_finish = jnp.where(s >= 1, (s - 1) // gb + 1, 0)
    combos = sorted({(q // gb + 1 if q < nb else 0, (q - 1) // gb + 1 if q >= 1 else 0) for q in range(nb + 1)})
    for a, b in combos:
        pl.when((ng_score == a) & (ng_finish == b))(functools.partial(step, a, b))


def _moba_attention(qk, vt, tab, batch, seq):
    blk = MOBA_BLOCK
    nb = seq // blk
    hp = MOBA_HEADS_PER_STEP
    w = hp * A_HEAD_DIM
    ngrp = A_HEADS // hp
    assert seq % blk == 0 and nb % MOBA_GROUP == 0 and A_HEADS % hp == 0
    kern = functools.partial(_moba_kernel, nb=nb)
    return pl.pallas_call(
        kern,
        grid=(batch, ngrp, nb + 1),
        in_specs=[
            pl.BlockSpec((blk, w), lambda b, h, s: (b * nb + jnp.minimum(s, nb - 1), h)),
            pl.BlockSpec((seq, w), lambda b, h, s: (b, ngrp + h)),
            pl.BlockSpec((None, hp * MOBA_VT_ROWS, seq), lambda b, h, s: (b, h, 0)),
            pl.BlockSpec((hp, 3, blk, blk), lambda b, h, s: (h, 0, 0, 0)),
        ],
        out_specs=pl.BlockSpec((blk, w), lambda b, h, s: (b * nb + jnp.maximum(s - 1, 0), h)),
        out_shape=jax.ShapeDtypeStruct((batch * seq, D_MODEL), BF16),
        scratch_shapes=[
            pltpu.VMEM((hp, nb, A_HEAD_DIM), F32),
            pltpu.VMEM((hp, nb, blk), F32),
            pltpu.VMEM((hp, seq, blk), F32),
            pltpu.VMEM((hp, seq, blk), BF16),
            pltpu.VMEM((hp, 1, blk), F32),
        ],
        compiler_params=_params("parallel", "parallel", "arbitrary"),
        name="moba_attn",
    )(qk, qk, vt, tab)


def _moba_layer(x, w_in, w_out, rel_bias, g, b, batch, seq):
    assert REL_MAX_DIST <= MOBA_BLOCK
    d = D_MODEL
    c1 = (A_HEAD_DIM ** -0.5) * LOG2E
    w_qk = jnp.concatenate([w_in[:, :d] * c1, w_in[:, d:2 * d]], axis=1).astype(BF16)
    w_vt = w_in[:, 2 * d:].T.astype(BF16)
    qk, vt = _moba_qkv(x, w_qk, w_vt, batch, seq)
    o = _moba_attention(qk, vt, _moba_bias_tables(rel_bias), batch, seq)
    return _proj_res_ln(o, w_out, x, g, b, "moba_out")


def _pool_kernel(x_ref, halo_ref, win_ref, wgrp_ref, scale_ref, wout_ref, g_ref, b_ref, o_ref,
                 ubuf_ref, ybuf_ref, *, tiles_per_seq):
    tm = x_ref.shape[0]
    hl = POOL_HALO
    ti = pl.program_id(0) % tiles_per_seq
    x = x_ref[...]
    u_halo = _dot(halo_ref[...].astype(BF16), win_ref[...])
    ubuf_ref[0:hl, :] = jnp.where(ti == 0, 0.0, u_halo)
    ubuf_ref[hl:hl + tm, :] = _dot(x.astype(BF16), win_ref[...])
    pos = ti * tm + lax.broadcasted_iota(jnp.int32, (tm, POOL_GROUP), 0)
    for gi, w in enumerate(POOL_WINDOWS):
        lo = gi * POOL_GROUP
        u = ubuf_ref[hl:hl + tm, lo:lo + POOL_GROUP]
        ws = u
        for d in range(1, w):
            ws = ws + ubuf_ref[hl - d:hl - d + tm, lo:lo + POOL_GROUP]
        cnt = jnp.minimum(pos + 1, w).astype(F32)
        pooled = ws / cnt - u
        yg = _dot(pooled.astype(BF16), wgrp_ref[gi]) * scale_ref[:, lo:lo + POOL_GROUP]
        ybuf_ref[:, lo:lo + POOL_GROUP] = yg.astype(BF16)
    y = _dot(ybuf_ref[...], wout_ref[...])
    o_ref[...] = _layer_norm(ALPHA * x + y, g_ref[...], b_ref[...])


def _pool_layer(x, w_in, w_group, scale, w_out, g, b, seq):
    t = x.shape[0]
    tm = ROW_TILE
    hl = POOL_HALO
    assert seq % tm == 0 and tm % hl == 0 and max(POOL_WINDOWS) <= hl
    kern = functools.partial(_pool_kernel, tiles_per_seq=seq // tm)
    ng = len(POOL_WINDOWS)
    return pl.pallas_call(
        kern,
        grid=(t // tm,),
        in_specs=[
            pl.BlockSpec((tm, D_MODEL), lambda i: (i, 0)),
            pl.BlockSpec((hl, D_MODEL), lambda i: (jnp.maximum(i * (tm // hl) - 1, 0), 0)),
            _const_spec((D_MODEL, D_MODEL)),
            _const_spec((ng, POOL_GROUP, POOL_GROUP)),
            _const_spec((1, D_MODEL)),
            _const_spec((D_MODEL, D_MODEL)),
            _const_spec((1, D_MODEL)),
            _const_spec((1, D_MODEL)),
        ],
        out_specs=pl.BlockSpec((tm, D_MODEL), lambda i: (i, 0)),
        out_shape=jax.ShapeDtypeStruct((t, D_MODEL), F32),
        scratch_shapes=[pltpu.VMEM((tm + hl, D_MODEL), F32), pltpu.VMEM((tm, D_MODEL), BF16)],
        compiler_params=_params("parallel"),
        name="pool_layer",
    )(x, x, w_in, w_group, scale, w_out, g, b)


def _log_sigmoid(x):
    return jnp.minimum(x, 0.0) - jnp.log1p(jnp.exp(-jnp.abs(x)))


def _mlstm_kernel(x_ref, qk_ref, v_ref, op_ref, wg_ref, wgt_ref, bg_row_ref, bg_col_ref, cw_ref,
                  ng_ref, wout_ref, g_ref, b_ref, o_ref,
                  c_ref, n_ref, m_ref, cprev_ref, cbuf_ref, hn_ref):
    L = C_CHUNK
    dh = C_HEAD_DIM
    nh = C_HEADS
    hl = CONV_HALO

    @pl.when(pl.program_id(1) == 0)
    def _():
        c_ref[...] = jnp.zeros_like(c_ref)
        n_ref[...] = jnp.zeros_like(n_ref)
        m_ref[...] = jnp.zeros_like(m_ref)
        cprev_ref[...] = jnp.zeros_like(cprev_ref)

    x = x_ref[...]
    xb = x.astype(BF16)
    g_col = _dot(xb, wg_ref[...]) + bg_row_ref[...]
    g_row = _dot_nt(wgt_ref[...], xb) + bg_col_ref[...]

    qk_pre = qk_ref[...]
    cbuf_ref[0:hl, :] = cprev_ref[...]
    cbuf_ref[hl:hl + L, :] = qk_pre
    cprev_ref[...] = qk_pre[L - hl:L, :]
    conv = cw_ref[C_CONV - 1:C_CONV, :] * qk_pre
    for j in range(C_CONV - 2, -1, -1):
        off = hl - (C_CONV - 1) + j
        conv = conv + cw_ref[j:j + 1, :] * cbuf_ref[off:off + L, :]
    qk = conv * _sigmoid(conv)

    r = lax.broadcasted_iota(jnp.int32, (L, L), 0)
    cc = lax.broadcasted_iota(jnp.int32, (L, L), 1)
    lower = cc <= r

    for hd in range(nh):
        i_col = g_col[:, hd:hd + 1]
        i_row = g_row[hd:hd + 1, :]
        lf_col = _log_sigmoid(g_col[:, nh + hd:nh + hd + 1])
        lf_row = _log_sigmoid(g_row[nh + hd:nh + hd + 1, :])
        b_col = jnp.sum(jnp.where(lower, lf_row, 0.0), axis=1, keepdims=True)
        b_row = jnp.sum(jnp.where(r <= cc, lf_col, 0.0), axis=0, keepdims=True)
        b_last = jnp.sum(lf_row, axis=1, keepdims=True)

        q_h = qk[:, hd * dh:(hd + 1) * dh]
        k_h = qk[:, D_MODEL + hd * dh:D_MODEL + (hd + 1) * dh] * (dh ** -0.5)
        qb = q_h.astype(BF16)
        kb = k_h.astype(BF16)
        vb = v_ref[:, hd * dh:(hd + 1) * dh].astype(BF16)
        c_st = c_ref[hd]
        n_st = n_ref[hd]
        m_prev = m_ref[hd]

        d_intra = jnp.where(lower, b_col - b_row + i_row, NEG)
        m_inter = b_col + m_prev
        m_t = jnp.maximum(m_inter, jnp.max(d_intra, axis=1, keepdims=True))
        w = jnp.exp(d_intra - m_t) * _dot_nt(qb, kb)
        s_inter = jnp.exp(m_inter - m_t)
        num = s_inter * _dot(qb, c_st.astype(BF16)) + _dot(w.astype(BF16), vb)
        den = s_inter * jnp.sum(q_h * n_st, axis=1, keepdims=True) + jnp.sum(w, axis=1, keepdims=True)
        ht = num / jnp.maximum(jnp.abs(den), jnp.exp(-m_t))

        gg_col = b_last - b_col + i_col
        gg_row = b_last - b_row + i_row
        m_new = jnp.maximum(b_last + m_prev, jnp.max(gg_row, axis=1, keepdims=True))
        decay = jnp.exp(b_last + m_prev - m_new)
        kw = k_h * jnp.exp(gg_col - m_new)
        c_ref[hd] = decay * c_st + _dot(jnp.transpose(kw).astype(BF16), vb)
        n_ref[hd] = decay * n_st + jnp.sum(kw, axis=0, keepdims=True)
        m_ref[hd] = m_new

        hc = _sigmoid(op_ref[:, hd * dh:(hd + 1) * dh]) * ht
        mu = jnp.mean(hc, axis=1, keepdims=True)
        hcc = hc - mu
        var = jnp.mean(hcc * hcc, axis=1, keepdims=True)
        hn = hcc * lax.rsqrt(var + LN_EPS) * ng_ref[:, hd * dh:(hd + 1) * dh]
        hn_ref[:, hd * dh:(hd + 1) * dh] = hn.astype(BF16)

    y = _dot(hn_ref[...], wout_ref[...])
    o_ref[...] = _layer_norm(ALPHA * x + y, g_ref[...], b_ref[...])


def _mlstm_layer(x, w_in, b_gates, conv_w, norm_g, w_out, g, b, batch, seq):
    t = x.shape[0]
    L = C_CHUNK
    nc = seq // L
    d = D_MODEL
    nh = C_HEADS
    assert seq % L == 0
    proj = _proj(x, w_in[:, :4 * d], F32, "mlstm_proj")
    w_gate = w_in[:, 4 * d:]
    wg = jnp.pad(w_gate, ((0, 0), (0, GATE_PAD - 2 * nh)))
    wgt = jnp.pad(w_gate.T, ((0, 16 - 2 * nh), (0, 0)))
    bg_row = jnp.pad(b_gates, (0, GATE_PAD - 2 * nh))[None, :].astype(F32)
    bg_col = jnp.pad(b_gates, (0, 16 - 2 * nh))[:, None].astype(F32)
    row = lambda bb, c: (bb * nc + c, 0)
    return pl.pallas_call(
        _mlstm_kernel,
        grid=(batch, nc),
        in_specs=[
            pl.BlockSpec((L, d), row),
            pl.BlockSpec((L, 2 * d), row),
            pl.BlockSpec((L, d), lambda bb, c: (bb * nc + c, 2)),
            pl.BlockSpec((L, d), lambda bb, c: (bb * nc + c, 3)),
            _const_spec((d, GATE_PAD)),
            _const_spec((16, d)),
            _const_spec((1, GATE_PAD)),
            _const_spec((16, 1)),
            _const_spec((C_CONV, 2 * d)),
            _const_spec((1, d)),
            _const_spec((d, d)),
            _const_spec((1, d)),
            _const_spec((1, d)),
        ],
        out_specs=pl.BlockSpec((L, d), row),
        out_shape=jax.ShapeDtypeStruct((t, d), F32),
        scratch_shapes=[
            pltpu.VMEM((nh, C_HEAD_DIM, C_HEAD_DIM), F32),
            pltpu.VMEM((nh, 1, C_HEAD_DIM), F32),
            pltpu.VMEM((nh, 1, 1), F32),
            pltpu.VMEM((CONV_HALO, 2 * d), F32),
            pltpu.VMEM((CONV_HALO + L, 2 * d), F32),
            pltpu.VMEM((L, d), BF16),
        ],
        compiler_params=_params("arbitrary", "arbitrary"),
        name="mlstm_layer",
    )(x, proj, proj, proj, wg.astype(BF16), wgt.astype(BF16), bg_row, bg_col, conv_w.astype(F32),
      norm_g[None, :].astype(F32), w_out, g, b)


def kernel(x, rel_bias, ln_g, ln_b, ffn_w_gu, ffn_w_down, a_w_in, a_w_out, b_w_in, b_w_group, b_scale, b_w_out,
           c_w_in, c_b_gates, c_conv_w, c_norm_g, c_w_out):
    batch, seq, d = x.shape
    h = x.reshape(batch * seq, d)
    bf = lambda w: w.astype(BF16)
    for i in range(DEPTH):
        lg = lambda s: ln_g[i, s][None, :]
        lb = lambda s: ln_b[i, s][None, :]
        h = _ffn(h, bf(ffn_w_gu[i, 0]), bf(ffn_w_down[i, 0]), lg(0), lb(0))
        kind, j = i % N_MIXERS, i // N_MIXERS
        if kind == 0:
            h = _moba_layer(h, a_w_in[j], bf(a_w_out[j]), rel_bias, lg(1), lb(1), batch, seq)
        elif kind == 1:
            h = _pool_layer(h, bf(b_w_in[j]), bf(b_w_group[j]), b_scale[j][None, :], bf(b_w_out[j]),
                            lg(1), lb(1), seq)
        else:
            h = _mlstm_layer(h, bf(c_w_in[j]), c_b_gates[j], c_conv_w[j], c_norm_g[j], bf(c_w_out[j]),
                             lg(1), lb(1), batch, seq)
        h = _ffn(h, bf(ffn_w_gu[i, 1]), bf(ffn_w_down[i, 1]), lg(2), lb(2))
    return h.reshape(batch, seq, d)
```

```python
import functools
import math

import numpy as np
import jax
import jax.numpy as jnp
from jax import lax
from jax.experimental import pallas as pl
from jax.experimental.pallas import tpu as pltpu

F32 = jnp.float32
BF16 = jnp.bfloat16

D_MODEL = 1024
DEPTH = 4
N_MIXERS = 3
D_FF = 2816
LN_EPS = 1e-5
ALPHA = (2 * DEPTH) ** 0.25
A_HEADS = 8
A_HEAD_DIM = D_MODEL // A_HEADS
MOBA_BLOCK = 256
MOBA_TOPK = 3
REL_BUCKETS = 32
REL_MAX_EXACT = REL_BUCKETS // 2
REL_MAX_DIST = 128
POOL_WINDOWS = (2, 4, 8, 16)
POOL_GROUP = D_MODEL // len(POOL_WINDOWS)
POOL_HALO = 16
C_HEADS = 4
C_HEAD_DIM = D_MODEL // C_HEADS
C_CONV = 4
C_CHUNK = 256
CONV_HALO = 8
GATE_PAD = 128

NEG = -1e30
LOG2E = math.log2(math.e)
MOBA_GROUP = 4
MOBA_HEADS_PER_STEP = 2
MOBA_EXP_ROWS = 64
MOBA_VT_ROWS = 128 + 16
V7X_VMEM_LIMIT = 56 * 1024 * 1024
FFN_CHUNK = 256
FFN_ROW_TILE = 1024
FFN_SUBTILE_ROWS = 256
ROW_TILE = 512

NT_DIMS = (((1,), (1,)), ((), ()))


def _params(*sem, flags=None):
    return pltpu.CompilerParams(dimension_semantics=sem, vmem_limit_bytes=V7X_VMEM_LIMIT, flags=flags)


def _const_spec(shape):
    nd = len(shape)
    return pl.BlockSpec(shape, lambda *_: (0,) * nd, pipeline_mode=pl.Buffered(1))


def _layer_norm(z, g, b):
    mu = jnp.mean(z, axis=-1, keepdims=True)
    zc = z - mu
    var = jnp.mean(zc * zc, axis=-1, keepdims=True)
    return zc * lax.rsqrt(var + LN_EPS) * g + b


def _sigmoid(x):
    return 1.0 / (1.0 + jnp.exp(-x))


def _dot(a, b):
    return jnp.dot(a, b, preferred_element_type=F32)


def _dot_nt(a, b):
    return lax.dot_general(a, b, NT_DIMS, preferred_element_type=F32)


def _ffn_chain_kernel(*refs, n_ffn, has_pre):
    x_ref = refs[0]
    n_in = 1 + 4 * has_pre + 4 * n_ffn
    o_ref, xb_ref, h_ref = refs[n_in:]
    sm = FFN_SUBTILE_ROWS
    for s in range(x_ref.shape[0] // sm):
        rows = slice(s * sm, (s + 1) * sm)
        slot = s % 2
        x = x_ref[rows, :]
        if has_pre:
            a_ref, wa_ref, g_ref, b_ref = refs[1:5]
            x = _layer_norm(ALPHA * x + _dot(a_ref[rows, :], wa_ref[...]), g_ref[...], b_ref[...])
        for f in range(n_ffn):
            wgu_ref, wd_ref, g_ref, b_ref = refs[1 + 4 * has_pre + 4 * f:5 + 4 * has_pre + 4 * f]
            xb_ref[slot] = x.astype(BF16)
            for c in range(D_FF // FFN_CHUNK):
                lo = c * FFN_CHUNK
                xb = xb_ref[slot]
                gate = _dot(xb, wgu_ref[:, lo:lo + FFN_CHUNK])
                up = _dot(xb, wgu_ref[:, D_FF + lo:D_FF + lo + FFN_CHUNK])
                h_ref[slot, :, lo:lo + FFN_CHUNK] = (gate * _sigmoid(gate) * up).astype(BF16)
            y = _dot(h_ref[slot], wd_ref[...])
            x = _layer_norm(ALPHA * x + 0.5 * y, g_ref[...], b_ref[...])
        o_ref[rows, :] = x


def _ffn_chain(x, w_gu_all, w_down_all, ln_g, ln_b, stages, pre=None):
    t = x.shape[0]
    n = len(stages)
    tm = FFN_ROW_TILE // n
    sm = FFN_SUBTILE_ROWS
    row = lambda i: (i, 0)
    in_specs = [pl.BlockSpec((tm, D_MODEL), row)]
    args = [x]
    if pre is not None:
        a, w_a, layer, ln_idx = pre
        in_specs += [pl.BlockSpec((tm, D_MODEL), row), _const_spec((D_MODEL, D_MODEL)),
                     _const_spec((1, D_MODEL)), _const_spec((1, D_MODEL))]
        args += [a, w_a, ln_g[layer, ln_idx][None, :], ln_b[layer, ln_idx][None, :]]
    for layer, slot, ln_idx in stages:
        pick = functools.partial(lambda l, s, *_: (l, s, 0, 0), layer, slot)
        in_specs += [
            pl.BlockSpec((None, None, D_MODEL, 2 * D_FF), pick, pipeline_mode=pl.Buffered(1)),
            pl.BlockSpec((None, None, D_FF, D_MODEL), pick, pipeline_mode=pl.Buffered(1)),
            _const_spec((1, D_MODEL)),
            _const_spec((1, D_MODEL)),
        ]
        args += [w_gu_all, w_down_all, ln_g[layer, ln_idx][None, :], ln_b[layer, ln_idx][None, :]]
    return pl.pallas_call(
        functools.partial(_ffn_chain_kernel, n_ffn=n, has_pre=pre is not None),
        grid=(t // tm,),
        in_specs=in_specs,
        out_specs=pl.BlockSpec((tm, D_MODEL), row),
        out_shape=jax.ShapeDtypeStruct((t, D_MODEL), F32),
        scratch_shapes=[pltpu.VMEM((2, sm, D_MODEL), BF16), pltpu.VMEM((2, sm, D_FF), BF16)],
        compiler_params=_params("parallel"),
        name="ffn_x%d%s" % (n, "_pre" if pre is not None else ""),
    )(*args)


def _proj_kernel(x_ref, w_ref, o_ref):
    o_ref[...] = _dot(x_ref[...].astype(BF16), w_ref[...]).astype(o_ref.dtype)


def _proj(x, w, out_dtype, name):
    t, k = x.shape
    n = w.shape[1]
    tm = ROW_TILE
    return pl.pallas_call(
        _proj_kernel,
        grid=(t // tm,),
        in_specs=[pl.BlockSpec((tm, k), lambda i: (i, 0)), _const_spec((k, n))],
        out_specs=pl.BlockSpec((tm, n), lambda i: (i, 0)),
        out_shape=jax.ShapeDtypeStruct((t, n), out_dtype),
        compiler_params=_params("parallel"),
        name=name,
    )(x, w)


def _t5_bucket(dist):
    n = jnp.maximum(dist, 0)
    is_small = n < REL_MAX_EXACT
    nf = jnp.maximum(n, 1).astype(F32)
    large = REL_MAX_EXACT + (jnp.log(nf / REL_MAX_EXACT) / math.log(REL_MAX_DIST / REL_MAX_EXACT)
                             * (REL_BUCKETS - REL_MAX_EXACT)).astype(jnp.int32)
    large = jnp.minimum(large, REL_BUCKETS - 1)
    return jnp.where(is_small, n, large)


def _moba_bias_tables(rel_bias):
    blk = MOBA_BLOCK
    nh = rel_bias.shape[1]
    far = rel_bias[REL_BUCKETS - 1][:, None]

    def by_distance(dist):
        onehot = _t5_bucket(dist)[:, None] == jnp.arange(REL_BUCKETS)
        picked = jnp.sum(jnp.where(onehot[None], rel_bias.T[:, None, :], 0.0), axis=-1)
        return (picked - far) * LOG2E

    def toeplitz(v):
        flat = jnp.broadcast_to(v[:, None, :], (nh, blk, 2 * blk)).reshape(nh, 2 * blk * blk)
        skew = flat[:, blk - 1:blk - 1 + blk * (2 * blk - 1)].reshape(nh, blk, 2 * blk - 1)
        return skew[:, :, :blk]

    d = jnp.arange(2 * blk) - (blk - 1)
    own = toeplitz(jnp.where(d[None] >= 0, by_distance(d), NEG))
    adj = toeplitz(by_distance(d + blk))
    return jnp.stack([own, adj, jnp.zeros_like(adj)], axis=1).astype(F32)


def _moba_qkv_kernel(x_ref, wqk_ref, wvt_ref, qk_ref, vt_ref):
    xb = x_ref[...].astype(BF16)
    qk_ref[...] = _dot(xb, wqk_ref[...]).astype(BF16)
    vt = _dot_nt(wvt_ref[...], xb).astype(BF16)
    dh, rows = A_HEAD_DIM, MOBA_VT_ROWS
    for h in range(A_HEADS):
        vt_ref[h * rows:h * rows + dh, :] = vt[h * dh:(h + 1) * dh, :]
        vt_ref[h * rows + dh:(h + 1) * rows, :] = jnp.ones((rows - dh, vt.shape[1]), BF16)


def _moba_qkv(x, w_qk, w_vt, batch, seq):
    t = x.shape[0]
    tm = ROW_TILE
    tps = seq // tm
    return pl.pallas_call(
        _moba_qkv_kernel,
        grid=(t // tm,),
        in_specs=[
            pl.BlockSpec((tm, D_MODEL), lambda i: (i, 0)),
            _const_spec((D_MODEL, 2 * D_MODEL)),
            _const_spec((D_MODEL, D_MODEL)),
        ],
        out_specs=[
            pl.BlockSpec((tm, 2 * D_MODEL), lambda i: (i, 0)),
            pl.BlockSpec((None, A_HEADS * MOBA_VT_ROWS, tm), lambda i: (i // tps, 0, i % tps)),
        ],
        out_shape=[
            jax.ShapeDtypeStruct((t, 2 * D_MODEL), BF16),
            jax.ShapeDtypeStruct((batch, A_HEADS * MOBA_VT_ROWS, seq), BF16),
        ],
        compiler_params=_params("parallel"),
        name="moba_qkv",
    )(x, w_qk, w_vt)


def _moba_kernel(q_ref, k_ref, vt_ref, tab_ref, o_ref, kmean_ref, sel_ref, t_ref, p_ref, m_ref, *, nb):
    blk = MOBA_BLOCK
    dh = A_HEAD_DIM
    gb = MOBA_GROUP
    rows = MOBA_VT_ROWS
    sub = MOBA_EXP_ROWS
    heads = range(MOBA_HEADS_PER_STEP)
    s = pl.program_id(2)

    @pl.when(s == 0)
    def _():
        for hh in heads:
            for j in range(nb):
                kb = k_ref[j * blk:(j + 1) * blk, hh * dh:(hh + 1) * dh].astype(F32)
                kmean_ref[hh, j:j + 1, :] = jnp.mean(kb, axis=0, keepdims=True)
        o_ref[...] = jnp.zeros_like(o_ref)

    def select():
        jidx = lax.broadcasted_iota(jnp.int32, (nb, blk), 0)
        for hh in heads:
            q = q_ref[:, hh * dh:(hh + 1) * dh]
            gate = _dot_nt(kmean_ref[hh].astype(BF16), q)
            cnt = jnp.zeros((nb, blk), F32)
            for jp in range(nb - 1):
                row = gate[jp:jp + 1, :]
                beats = (row > gate) | ((row == gate) & (jp < jidx))
                cnt = cnt + jnp.where(beats & (jp < s), 1.0, 0.0)
            chosen = ((cnt < MOBA_TOPK) & (jidx < s)) | (jidx == s)
            sel_ref[hh] = jnp.where(chosen, 1.0, 0.0)

    def step(ng_score, ng_finish):
        first_dynamic = (ng_score - 1) * gb - 1
        nk_finish = ng_finish * gb * blk
        if ng_finish:
            m_prev = [m_ref[hh] for hh in heads]
        if ng_score:
            select()
        m8 = [None] * len(heads)
        for j in range(max(ng_score, ng_finish) * gb):
            if j < ng_finish * gb:
                for hh in heads:
                    for r in range(j * blk, (j + 1) * blk, sub):
                        p_ref[hh, r:r + sub, :] = jnp.exp2(t_ref[hh, r:r + sub, :] - m_prev[hh]).astype(BF16)
            if j < ng_score * gb:
                for hh in heads:
                    t = _dot_nt(k_ref[j * blk:(j + 1) * blk, hh * dh:(hh + 1) * dh],
                                q_ref[:, hh * dh:(hh + 1) * dh])
                    if j >= first_dynamic:
                        t = t + tab_ref[hh, jnp.clip(s - j, 0, 2)]
                    t = jnp.where(sel_ref[hh, j:j + 1, :] > 0.5, t, NEG)
                    t_ref[hh, j * blk:(j + 1) * blk, :] = t
                    mb = jnp.max(t.reshape(blk // 8, 8, blk), axis=0)
                    m8[hh] = mb if j == 0 else jnp.maximum(m8[hh], mb)
        if ng_score:
            for hh in heads:
                m_ref[hh] = jnp.max(m8[hh], axis=0, keepdims=True)
        if ng_finish:
            for hh in heads:
                o_aug = _dot(vt_ref[hh * rows:(hh + 1) * rows, 0:nk_finish], p_ref[hh, 0:nk_finish, :])
                o_t = o_aug[0:dh, :] / o_aug[dh:dh + 1, :]
                o_ref[:, hh * dh:(hh + 1) * dh] = jnp.transpose(o_t).astype(o_ref.dtype)

    ng_score = jnp.where(s < nb, s // gb + 1, 0)
    ng_finish = jnp.where(s >= 1, (s - 1) // gb + 1, 0)
    combos = sorted({(q // gb + 1 if q < nb else 0, (q - 1) // gb + 1 if q >= 1 else 0) for q in range(nb + 1)})
    for a, b in combos:
        pl.when((ng_score == a) & (ng_finish == b))(functools.partial(step, a, b))


def _moba_attention(qk, vt, tab, batch, seq):
    blk = MOBA_BLOCK
    nb = seq // blk
    hp = MOBA_HEADS_PER_STEP
    w = hp * A_HEAD_DIM
    ngrp = A_HEADS // hp
    assert seq % blk == 0 and nb % MOBA_GROUP == 0 and A_HEADS % hp == 0
    kern = functools.partial(_moba_kernel, nb=nb)
    return pl.pallas_call(
        kern,
        grid=(batch, ngrp, nb + 1),
        in_specs=[
            pl.BlockSpec((blk, w), lambda b, h, s: (b * nb + jnp.minimum(s, nb - 1), h)),
            pl.BlockSpec((seq, w), lambda b, h, s: (b, ngrp + h)),
            pl.BlockSpec((None, hp * MOBA_VT_ROWS, seq), lambda b, h, s: (b, h, 0)),
            pl.BlockSpec((hp, 3, blk, blk), lambda b, h, s: (h, 0, 0, 0)),
        ],
        out_specs=pl.BlockSpec((blk, w), lambda b, h, s: (b * nb + jnp.maximum(s - 1, 0), h)),
        out_shape=jax.ShapeDtypeStruct((batch * seq, D_MODEL), BF16),
        scratch_shapes=[
            pltpu.VMEM((hp, nb, A_HEAD_DIM), F32),
            pltpu.VMEM((hp, nb, blk), F32),
            pltpu.VMEM((hp, seq, blk), F32),
            pltpu.VMEM((hp, seq, blk), BF16),
            pltpu.VMEM((hp, 1, blk), F32),
        ],
        compiler_params=_params("parallel", "parallel", "arbitrary"),
        name="moba_attn",
    )(qk, qk, vt, tab)


def _moba_mixer(x, w_in, rel_bias, batch, seq):
    assert REL_MAX_DIST <= MOBA_BLOCK
    d = D_MODEL
    c1 = (A_HEAD_DIM ** -0.5) * LOG2E
    w_qk = jnp.concatenate([w_in[:, :d] * c1, w_in[:, d:2 * d]], axis=1).astype(BF16)
    w_vt = w_in[:, 2 * d:].T.astype(BF16)
    qk, vt = _moba_qkv(x, w_qk, w_vt, batch, seq)
    return _moba_attention(qk, vt, _moba_bias_tables(rel_bias), batch, seq)


def _pool_kernel(x_ref, halo_ref, win_ref, wgrp_ref, scale_ref, wout_ref, g_ref, b_ref, o_ref,
                 ubuf_ref, ybuf_ref, *, tiles_per_seq):
    tm = x_ref.shape[0]
    hl = POOL_HALO
    ti = pl.program_id(0) % tiles_per_seq
    x = x_ref[...]
    u_halo = _dot(halo_ref[...].astype(BF16), win_ref[...])
    ubuf_ref[0:hl, :] = jnp.where(ti == 0, 0.0, u_halo)
    ubuf_ref[hl:hl + tm, :] = _dot(x.astype(BF16), win_ref[...])
    pos = ti * tm + lax.broadcasted_iota(jnp.int32, (tm, POOL_GROUP), 0)
    for gi, w in enumerate(POOL_WINDOWS):
        lo = gi * POOL_GROUP
        u = ubuf_ref[hl:hl + tm, lo:lo + POOL_GROUP]
        ws = u
        for d in range(1, w):
            ws = ws + ubuf_ref[hl - d:hl - d + tm, lo:lo + POOL_GROUP]
        cnt = jnp.minimum(pos + 1, w).astype(F32)
        pooled = ws / cnt - u
        yg = _dot(pooled.astype(BF16), wgrp_ref[gi]) * scale_ref[:, lo:lo + POOL_GROUP]
        ybuf_ref[:, lo:lo + POOL_GROUP] = yg.astype(BF16)
    y = _dot(ybuf_ref[...], wout_ref[...])
    o_ref[...] = _layer_norm(ALPHA * x + y, g_ref[...], b_ref[...])


def _pool_layer(x, w_in, w_group, scale, w_out, g, b, seq):
    t = x.shape[0]
    tm = ROW_TILE
    hl = POOL_HALO
    assert seq % tm == 0 and tm % hl == 0 and max(POOL_WINDOWS) <= hl
    kern = functools.partial(_pool_kernel, tiles_per_seq=seq // tm)
    ng = len(POOL_WINDOWS)
    return pl.pallas_call(
        kern,
        grid=(t // tm,),
        in_specs=[
            pl.BlockSpec((tm, D_MODEL), lambda i: (i, 0)),
            pl.BlockSpec((hl, D_MODEL), lambda i: (jnp.maximum(i * (tm // hl) - 1, 0), 0)),
            _const_spec((D_MODEL, D_MODEL)),
            _const_spec((ng, POOL_GROUP, POOL_GROUP)),
            _const_spec((1, D_MODEL)),
            _const_spec((D_MODEL, D_MODEL)),
            _const_spec((1, D_MODEL)),
            _const_spec((1, D_MODEL)),
        ],
        out_specs=pl.BlockSpec((tm, D_MODEL), lambda i: (i, 0)),
        out_shape=jax.ShapeDtypeStruct((t, D_MODEL), F32),
        scratch_shapes=[pltpu.VMEM((tm + hl, D_MODEL), F32), pltpu.VMEM((tm, D_MODEL), BF16)],
        compiler_params=_params("parallel"),
        name="pool_layer",
    )(x, x, w_in, w_group, scale, w_out, g, b)


def _log_sigmoid(x):
    return jnp.minimum(x, 0.0) - jnp.log1p(jnp.exp(-jnp.abs(x)))


def _mlstm_kernel(x_ref, qk_ref, v_ref, op_ref, wg_ref, wgt_ref, bg_row_ref, bg_col_ref, cw_ref,
                  ng_ref, wout_ref, g_ref, b_ref, o_ref,
                  c_ref, n_ref, m_ref, cprev_ref, cbuf_ref, hn_ref):
    L = C_CHUNK
    dh = C_HEAD_DIM
    nh = C_HEADS
    hl = CONV_HALO

    @pl.when(pl.program_id(1) == 0)
    def _():
        c_ref[...] = jnp.zeros_like(c_ref)
        n_ref[...] = jnp.zeros_like(n_ref)
        m_ref[...] = jnp.zeros_like(m_ref)
        cprev_ref[...] = jnp.zeros_like(cprev_ref)

    x = x_ref[...]
    xb = x.astype(BF16)
    g_col = _dot(xb, wg_ref[...]) + bg_row_ref[...]
    g_row = _dot_nt(wgt_ref[...], xb) + bg_col_ref[...]

    qk_pre = qk_ref[...]
    cbuf_ref[0:hl, :] = cprev_ref[...]
    cbuf_ref[hl:hl + L, :] = qk_pre
    cprev_ref[...] = qk_pre[L - hl:L, :]
    conv = cw_ref[C_CONV - 1:C_CONV, :] * qk_pre
    for j in range(C_CONV - 2, -1, -1):
        off = hl - (C_CONV - 1) + j
        conv = conv + cw_ref[j:j + 1, :] * cbuf_ref[off:off + L, :]
    qk = conv * _sigmoid(conv)

    r = lax.broadcasted_iota(jnp.int32, (L, L), 0)
    cc = lax.broadcasted_iota(jnp.int32, (L, L), 1)
    lower = cc <= r

    for hd in range(nh):
        i_col = g_col[:, hd:hd + 1]
        i_row = g_row[hd:hd + 1, :]
        lf_col = _log_sigmoid(g_col[:, nh + hd:nh + hd + 1])
        lf_row = _log_sigmoid(g_row[nh + hd:nh + hd + 1, :])
        b_col = jnp.sum(jnp.where(lower, lf_row, 0.0), axis=1, keepdims=True)
        b_row = jnp.sum(jnp.where(r <= cc, lf_col, 0.0), axis=0, keepdims=True)
        b_last = jnp.sum(lf_row, axis=1, keepdims=True)

        q_h = qk[:, hd * dh:(hd + 1) * dh]
        k_h = qk[:, D_MODEL + hd * dh:D_MODEL + (hd + 1) * dh] * (dh ** -0.5)
        qb = q_h.astype(BF16)
        kb = k_h.astype(BF16)
        vb = v_ref[:, hd * dh:(hd + 1) * dh].astype(BF16)
        c_st = c_ref[hd]
        n_st = n_ref[hd]
        m_prev = m_ref[hd]

        d_intra = jnp.where(lower, b_col - b_row + i_row, NEG)
        m_inter = b_col + m_prev
        m_t = jnp.maximum(m_inter, jnp.max(d_intra, axis=1, keepdims=True))
        w = jnp.exp(d_intra - m_t) * _dot_nt(qb, kb)
        s_inter = jnp.exp(m_inter - m_t)
        num = s_inter * _dot(qb, c_st.astype(BF16)) + _dot(w.astype(BF16), vb)
        den = s_inter * jnp.sum(q_h * n_st, axis=1, keepdims=True) + jnp.sum(w, axis=1, keepdims=True)
        ht = num / jnp.maximum(jnp.abs(den), jnp.exp(-m_t))

        gg_col = b_last - b_col + i_col
        gg_row = b_last - b_row + i_row
        m_new = jnp.maximum(b_last + m_prev, jnp.max(gg_row, axis=1, keepdims=True))
        decay = jnp.exp(b_last + m_prev - m_new)
        kw = k_h * jnp.exp(gg_col - m_new)
        c_ref[hd] = decay * c_st + _dot(jnp.transpose(kw).astype(BF16), vb)
        n_ref[hd] = decay * n_st + jnp.sum(kw, axis=0, keepdims=True)
        m_ref[hd] = m_new

        hc = _sigmoid(op_ref[:, hd * dh:(hd + 1) * dh]) * ht
        mu = jnp.mean(hc, axis=1, keepdims=True)
        hcc = hc - mu
        var = jnp.mean(hcc * hcc, axis=1, keepdims=True)
        hn = hcc * lax.rsqrt(var + LN_EPS) * ng_ref[:, hd * dh:(hd + 1) * dh]
        hn_ref[:, hd * dh:(hd + 1) * dh] = hn.astype(BF16)

    y = _dot(hn_ref[...], wout_ref[...])
    o_ref[...] = _layer_norm(ALPHA * x + y, g_ref[...], b_ref[...])


def _mlstm_layer(x, w_in, b_gates, conv_w, norm_g, w_out, g, b, batch, seq):
    t = x.shape[0]
    L = C_CHUNK
    nc = seq // L
    d = D_MODEL
    nh = C_HEADS
    assert seq % L == 0
    proj = _proj(x, w_in[:, :4 * d], F32, "mlstm_proj")
    w_gate = w_in[:, 4 * d:]
    wg = jnp.pad(w_gate, ((0, 0), (0, GATE_PAD - 2 * nh)))
    wgt = jnp.pad(w_gate.T, ((0, 16 - 2 * nh), (0, 0)))
    bg_row = jnp.pad(b_gates, (0, GATE_PAD - 2 * nh))[None, :].astype(F32)
    bg_col = jnp.pad(b_gates, (0, 16 - 2 * nh))[:, None].astype(F32)
    row = lambda bb, c: (bb * nc + c, 0)
    return pl.pallas_call(
        _mlstm_kernel,
        grid=(batch, nc),
        in_specs=[
            pl.BlockSpec((L, d), row),
            pl.BlockSpec((L, 2 * d), row),
            pl.BlockSpec((L, d), lambda bb, c: (bb * nc + c, 2)),
            pl.BlockSpec((L, d), lambda bb, c: (bb * nc + c, 3)),
            _const_spec((d, GATE_PAD)),
            _const_spec((16, d)),
            _const_spec((1, GATE_PAD)),
            _const_spec((16, 1)),
            _const_spec((C_CONV, 2 * d)),
            _const_spec((1, d)),
            _const_spec((d, d)),
            _const_spec((1, d)),
            _const_spec((1, d)),
        ],
        out_specs=pl.BlockSpec((L, d), row),
        out_shape=jax.ShapeDtypeStruct((t, d), F32),
        scratch_shapes=[
            pltpu.VMEM((nh, C_HEAD_DIM, C_HEAD_DIM), F32),
            pltpu.VMEM((nh, 1, C_HEAD_DIM), F32),
            pltpu.VMEM((nh, 1, 1), F32),
            pltpu.VMEM((CONV_HALO, 2 * d), F32),
            pltpu.VMEM((CONV_HALO + L, 2 * d), F32),
            pltpu.VMEM((L, d), BF16),
        ],
        compiler_params=_params("arbitrary", "arbitrary"),
        name="mlstm_layer",
    )(x, proj, proj, proj, wg.astype(BF16), wgt.astype(BF16), bg_row, bg_col, conv_w.astype(F32),
      norm_g[None, :].astype(F32), w_out, g, b)


def kernel(x, rel_bias, ln_g, ln_b, ffn_w_gu, ffn_w_down, a_w_in, a_w_out, b_w_in, b_w_group, b_scale, b_w_out,
           c_w_in, c_b_gates, c_conv_w, c_norm_g, c_w_out):
    batch, seq, d = x.shape
    h = x.reshape(batch * seq, d)
    bf = lambda w: w.astype(BF16)
    w_gu_all, w_down_all = bf(ffn_w_gu), bf(ffn_w_down)
    pre = None
    for i in range(DEPTH):
        lg = lambda s: ln_g[i, s][None, :]
        lb = lambda s: ln_b[i, s][None, :]
        stages = [(i, 0, 0)] if i == 0 else [(i - 1, 1, 2), (i, 0, 0)]
        h = _ffn_chain(h, w_gu_all, w_down_all, ln_g, ln_b, stages, pre)
        pre = None
        kind, j = i % N_MIXERS, i // N_MIXERS
        if kind == 0:
            pre = (_moba_mixer(h, a_w_in[j], rel_bias, batch, seq), bf(a_w_out[j]), i, 1)
        elif kind == 1:
            h = _pool_layer(h, bf(b_w_in[j]), bf(b_w_group[j]), b_scale[j][None, :], bf(b_w_out[j]),
                            lg(1), lb(1), seq)
        else:
            h = _mlstm_layer(h, bf(c_w_in[j]), c_b_gates[j], c_conv_w[j], c_norm_g[j], bf(c_w_out[j]),
                             lg(1), lb(1), batch, seq)
    h = _ffn_chain(h, w_gu_all, w_down_all, ln_g, ln_b, [(DEPTH - 1, 1, 2)], pre)
    return h.reshape(batch, seq, d)
```

```python
import functools
import math

import jax
import jax.numpy as jnp
from jax import lax
from jax.experimental import pallas as pl
from jax.experimental.pallas import tpu as pltpu

F32 = jnp.float32
BF16 = jnp.bfloat16

D_MODEL = 1024
DEPTH = 4
N_MIXERS = 3
D_FF = 2816
LN_EPS = 1e-5
ALPHA = (2 * DEPTH) ** 0.25
A_HEADS = 8
A_HEAD_DIM = D_MODEL // A_HEADS
MOBA_BLOCK = 256
MOBA_TOPK = 3
REL_BUCKETS = 32
REL_MAX_EXACT = REL_BUCKETS // 2
REL_MAX_DIST = 128
POOL_WINDOWS = (2, 4, 8, 16)
POOL_GROUP = D_MODEL // len(POOL_WINDOWS)
POOL_HALO = 16
C_HEADS = 4
C_HEAD_DIM = D_MODEL // C_HEADS
C_CONV = 4
C_CHUNK = 256
CONV_HALO = 8
GATE_PAD = 128

NEG = -1e30
LOG2E = math.log2(math.e)
MOBA_GROUP = 4
MOBA_HEADS_PER_STEP = 2
MOBA_EXP_ROWS = 64
MOBA_VT_ROWS = 128 + 16
V7X_VMEM_LIMIT = 56 * 1024 * 1024
FFN_CHUNK = 256
FFN_ROW_TILE = 1024
FFN_SUBTILE_ROWS = 256
ROW_TILE = 512

NT_DIMS = (((1,), (1,)), ((), ()))


def _params(*sem):
    return pltpu.CompilerParams(dimension_semantics=sem, vmem_limit_bytes=V7X_VMEM_LIMIT)


def _const_spec(shape):
    nd = len(shape)
    return pl.BlockSpec(shape, lambda *_: (0,) * nd, pipeline_mode=pl.Buffered(1))


def _layer_norm(z, g, b):
    mu = jnp.mean(z, axis=-1, keepdims=True)
    zc = z - mu
    var = jnp.mean(zc * zc, axis=-1, keepdims=True)
    return zc * lax.rsqrt(var + LN_EPS) * g + b


def _sigmoid(x):
    return 1.0 / (1.0 + jnp.exp(-x))


def _dot(a, b):
    return jnp.dot(a, b, preferred_element_type=F32)


def _dot_nt(a, b):
    return lax.dot_general(a, b, NT_DIMS, preferred_element_type=F32)


def _ffn_kernel(*refs, has_pre):
    x_ref = refs[0]
    wgu_ref, wd_ref, g_ref, b_ref, o_ref, xb_ref, h_ref = refs[1 + 4 * has_pre:]
    sm = FFN_SUBTILE_ROWS
    for s in range(x_ref.shape[0] // sm):
        rows = slice(s * sm, (s + 1) * sm)
        slot = s % 2
        x = x_ref[rows, :]
        if has_pre:
            a_ref, wa_ref, ga_ref, ba_ref = refs[1:5]
            x = _layer_norm(ALPHA * x + _dot(a_ref[rows, :], wa_ref[...]), ga_ref[...], ba_ref[...])
        xb_ref[slot] = x.astype(BF16)
        for c in range(D_FF // FFN_CHUNK):
            lo = c * FFN_CHUNK
            xb = xb_ref[slot]
            gate = _dot(xb, wgu_ref[:, lo:lo + FFN_CHUNK])
            up = _dot(xb, wgu_ref[:, D_FF + lo:D_FF + lo + FFN_CHUNK])
            h_ref[slot, :, lo:lo + FFN_CHUNK] = (gate * _sigmoid(gate) * up).astype(BF16)
        y = _dot(h_ref[slot], wd_ref[...])
        o_ref[rows, :] = _layer_norm(ALPHA * x + 0.5 * y, g_ref[...], b_ref[...])


def _ffn(x, w_gu_all, w_down_all, layer, slot, g, b, pre=None):
    t = x.shape[0]
    tm = FFN_ROW_TILE
    sm = FFN_SUBTILE_ROWS
    row = lambda i: (i, 0)
    pick = lambda *_: (layer, slot, 0, 0)
    vec = _const_spec((1, D_MODEL))
    in_specs = [pl.BlockSpec((tm, D_MODEL), row)]
    args = [x]
    if pre is not None:
        in_specs += [pl.BlockSpec((tm, D_MODEL), row), _const_spec((D_MODEL, D_MODEL)), vec, vec]
        args += list(pre)
    in_specs += [
        pl.BlockSpec((None, None, D_MODEL, 2 * D_FF), pick, pipeline_mode=pl.Buffered(1)),
        pl.BlockSpec((None, None, D_FF, D_MODEL), pick, pipeline_mode=pl.Buffered(1)),
        vec, vec,
    ]
    args += [w_gu_all, w_down_all, g, b]
    return pl.pallas_call(
        functools.partial(_ffn_kernel, has_pre=pre is not None),
        grid=(t // tm,),
        in_specs=in_specs,
        out_specs=pl.BlockSpec((tm, D_MODEL), row),
        out_shape=jax.ShapeDtypeStruct((t, D_MODEL), F32),
        scratch_shapes=[pltpu.VMEM((2, sm, D_MODEL), BF16), pltpu.VMEM((2, sm, D_FF), BF16)],
        compiler_params=_params("parallel"),
        name="ffn_pre" if pre is not None else "ffn",
    )(*args)


def _t5_bucket(dist):
    n = jnp.maximum(dist, 0)
    is_small = n < REL_MAX_EXACT
    nf = jnp.maximum(n, 1).astype(F32)
    large = REL_MAX_EXACT + (jnp.log(nf / REL_MAX_EXACT) / math.log(REL_MAX_DIST / REL_MAX_EXACT)
                             * (REL_BUCKETS - REL_MAX_EXACT)).astype(jnp.int32)
    large = jnp.minimum(large, REL_BUCKETS - 1)
    return jnp.where(is_small, n, large)


def _moba_bias_tables(rel_bias):
    blk = MOBA_BLOCK
    nh = rel_bias.shape[1]
    far = rel_bias[REL_BUCKETS - 1][:, None]

    def by_distance(dist):
        onehot = _t5_bucket(dist)[:, None] == jnp.arange(REL_BUCKETS)
        picked = jnp.sum(jnp.where(onehot[None], rel_bias.T[:, None, :], 0.0), axis=-1)
        return (picked - far) * LOG2E

    def toeplitz(v):
        flat = jnp.broadcast_to(v[:, None, :], (nh, blk, 2 * blk)).reshape(nh, 2 * blk * blk)
        skew = flat[:, blk - 1:blk - 1 + blk * (2 * blk - 1)].reshape(nh, blk, 2 * blk - 1)
        return skew[:, :, :blk]

    d = jnp.arange(2 * blk) - (blk - 1)
    own = toeplitz(jnp.where(d[None] >= 0, by_distance(d), NEG))
    adj = toeplitz(by_distance(d + blk))
    return jnp.stack([own, adj, jnp.zeros_like(adj)], axis=1).astype(F32)


def _moba_qkv_kernel(x_ref, wqk_ref, wvt_ref, qk_ref, vt_ref):
    xb = x_ref[...].astype(BF16)
    qk_ref[...] = _dot(xb, wqk_ref[...]).astype(BF16)
    vt = _dot_nt(wvt_ref[...], xb).astype(BF16)
    dh, rows = A_HEAD_DIM, MOBA_VT_ROWS
    for h in range(A_HEADS):
        vt_ref[h * rows:h * rows + dh, :] = vt[h * dh:(h + 1) * dh, :]
        vt_ref[h * rows + dh:(h + 1) * rows, :] = jnp.ones((rows - dh, vt.shape[1]), BF16)


def _moba_qkv(x, w_qk, w_vt, batch, seq):
    t = x.shape[0]
    tm = ROW_TILE
    tps = seq // tm
    return pl.pallas_call(
        _moba_qkv_kernel,
        grid=(t // tm,),
        in_specs=[
            pl.BlockSpec((tm, D_MODEL), lambda i: (i, 0)),
            _const_spec((D_MODEL, 2 * D_MODEL)),
            _const_spec((D_MODEL, D_MODEL)),
        ],
        out_specs=[
            pl.BlockSpec((tm, 2 * D_MODEL), lambda i: (i, 0)),
            pl.BlockSpec((None, A_HEADS * MOBA_VT_ROWS, tm), lambda i: (i // tps, 0, i % tps)),
        ],
        out_shape=[
            jax.ShapeDtypeStruct((t, 2 * D_MODEL), BF16),
            jax.ShapeDtypeStruct((batch, A_HEADS * MOBA_VT_ROWS, seq), BF16),
        ],
        compiler_params=_params("parallel"),
        name="moba_qkv",
    )(x, w_qk, w_vt)


def _moba_kernel(q_ref, k_ref, vt_ref, tab_ref, o_ref, kmean_ref, sel_ref, t_ref, p_ref, m_ref, *, nb):
    blk = MOBA_BLOCK
    dh = A_HEAD_DIM
    gb = MOBA_GROUP
    rows = MOBA_VT_ROWS
    sub = MOBA_EXP_ROWS
    heads = range(MOBA_HEADS_PER_STEP)
    s = pl.program_id(2)

    @pl.when(s == 0)
    def _():
        for hh in heads:
            for j in range(nb):
                kb = k_ref[j * blk:(j + 1) * blk, hh * dh:(hh + 1) * dh].astype(F32)
                kmean_ref[hh, j:j + 1, :] = jnp.mean(kb, axis=0, keepdims=True)
        o_ref[...] = jnp.zeros_like(o_ref)

    def select():
        jidx = lax.broadcasted_iota(jnp.int32, (nb, blk), 0)
        for hh in heads:
            q = q_ref[:, hh * dh:(hh + 1) * dh]
            gate = _dot_nt(kmean_ref[hh].astype(BF16), q)
            cnt = jnp.zeros((nb, blk), F32)
            for jp in range(nb - 1):
                row = gate[jp:jp + 1, :]
                beats = (row > gate) | ((row == gate) & (jp < jidx))
                cnt = cnt + jnp.where(beats & (jp < s), 1.0, 0.0)
            chosen = ((cnt < MOBA_TOPK) & (jidx < s)) | (jidx == s)
            sel_ref[hh] = jnp.where(chosen, 1.0, 0.0)

    def step(ng_score, ng_finish):
        first_dynamic = (ng_score - 1) * gb - 1
        nk_finish = ng_finish * gb * blk
        if ng_finish:
            m_prev = [m_ref[hh] for hh in heads]
        if ng_score:
            select()
        m8 = [None] * len(heads)
        for j in range(max(ng_score, ng_finish) * gb):
            if j < ng_finish * gb:
                for hh in heads:
                    for r in range(j * blk, (j + 1) * blk, sub):
                        p_ref[hh, r:r + sub, :] = jnp.exp2(t_ref[hh, r:r + sub, :] - m_prev[hh]).astype(BF16)
            if j < ng_score * gb:
                for hh in heads:
                    t = _dot_nt(k_ref[j * blk:(j + 1) * blk, hh * dh:(hh + 1) * dh],
                                q_ref[:, hh * dh:(hh + 1) * dh])
                    if j >= first_dynamic:
                        t = t + tab_ref[hh, jnp.clip(s - j, 0, 2)]
                    t = jnp.where(sel_ref[hh, j:j + 1, :] > 0.5, t, NEG)
                    t_ref[hh, j * blk:(j + 1) * blk, :] = t
                    mb = jnp.max(t.reshape(blk // 8, 8, blk), axis=0)
                    m8[hh] = mb if j == 0 else jnp.maximum(m8[hh], mb)
        if ng_score:
            for hh in heads:
                m_ref[hh] = jnp.max(m8[hh], axis=0, keepdims=True)
        if ng_finish:
            for hh in heads:
                o_aug = _dot(vt_ref[hh * rows:(hh + 1) * rows, 0:nk_finish], p_ref[hh, 0:nk_finish, :])
                o_t = o_aug[0:dh, :] / o_aug[dh:dh + 1, :]
                o_ref[:, hh * dh:(hh + 1) * dh] = jnp.transpose(o_t).astype(o_ref.dtype)

    ng_score = jnp.where(s < nb, s // gb + 1, 0)
    ng_finish = jnp.where(s >= 1, (s - 1) // gb + 1, 0)
    combos = sorted({(q // gb + 1 if q < nb else 0, (q - 1) // gb + 1 if q >= 1 else 0) for q in range(nb + 1)})
    for a, b in combos:
        pl.when((ng_score == a) & (ng_finish == b))(functools.partial(step, a, b))


def _moba_attention(qk, vt, tab, batch, seq):
    blk = MOBA_BLOCK
    nb = seq // blk
    hp = MOBA_HEADS_PER_STEP
    w = hp * A_HEAD_DIM
    ngrp = A_HEADS // hp
    assert seq % blk == 0 and nb % MOBA_GROUP == 0 and A_HEADS % hp == 0
    kern = functools.partial(_moba_kernel, nb=nb)
    return pl.pallas_call(
        kern,
        grid=(batch, ngrp, nb + 1),
        in_specs=[
            pl.BlockSpec((blk, w), lambda b, h, s: (b * nb + jnp.minimum(s, nb - 1), h)),
            pl.BlockSpec((seq, w), lambda b, h, s: (b, ngrp + h)),
            pl.BlockSpec((None, hp * MOBA_VT_ROWS, seq), lambda b, h, s: (b, h, 0)),
            pl.BlockSpec((hp, 3, blk, blk), lambda b, h, s: (h, 0, 0, 0)),
        ],
        out_specs=pl.BlockSpec((blk, w), lambda b, h, s: (b * nb + jnp.maximum(s - 1, 0), h)),
        out_shape=jax.ShapeDtypeStruct((batch * seq, D_MODEL), BF16),
        scratch_shapes=[
            pltpu.VMEM((hp, nb, A_HEAD_DIM), F32),
            pltpu.VMEM((hp, nb, blk), F32),
            pltpu.VMEM((hp, seq, blk), F32),
            pltpu.VMEM((hp, seq, blk), BF16),
            pltpu.VMEM((hp, 1, blk), F32),
        ],
        compiler_params=_params("parallel", "parallel", "arbitrary"),
        name="moba_attn",
    )(qk, qk, vt, tab)


def _moba_mixer(x, w_in, rel_bias, batch, seq):
    assert REL_MAX_DIST <= MOBA_BLOCK
    d = D_MODEL
    c1 = (A_HEAD_DIM ** -0.5) * LOG2E
    w_qk = jnp.concatenate([w_in[:, :d] * c1, w_in[:, d:2 * d]], axis=1).astype(BF16)
    w_vt = w_in[:, 2 * d:].T.astype(BF16)
    qk, vt = _moba_qkv(x, w_qk, w_vt, batch, seq)
    return _moba_attention(qk, vt, _moba_bias_tables(rel_bias), batch, seq)


def _pool_kernel(x_ref, halo_ref, win_ref, wgrp_ref, scale_ref, wout_ref, g_ref, b_ref, o_ref,
                 ubuf_ref, ybuf_ref, *, tiles_per_seq):
    tm = x_ref.shape[0]
    hl = POOL_HALO
    ti = pl.program_id(0) % tiles_per_seq
    x = x_ref[...]
    u_halo = _dot(halo_ref[...].astype(BF16), win_ref[...])
    ubuf_ref[0:hl, :] = jnp.where(ti == 0, 0.0, u_halo)
    ubuf_ref[hl:hl + tm, :] = _dot(x.astype(BF16), win_ref[...])
    pos = ti * tm + lax.broadcasted_iota(jnp.int32, (tm, POOL_GROUP), 0)
    for gi, w in enumerate(POOL_WINDOWS):
        lo = gi * POOL_GROUP
        u = ubuf_ref[hl:hl + tm, lo:lo + POOL_GROUP]
        ws = u
        for d in range(1, w):
            ws = ws + ubuf_ref[hl - d:hl - d + tm, lo:lo + POOL_GROUP]
        cnt = jnp.minimum(pos + 1, w).astype(F32)
        pooled = ws / cnt - u
        yg = _dot(pooled.astype(BF16), wgrp_ref[gi]) * scale_ref[:, lo:lo + POOL_GROUP]
        ybuf_ref[:, lo:lo + POOL_GROUP] = yg.astype(BF16)
    y = _dot(ybuf_ref[...], wout_ref[...])
    o_ref[...] = _layer_norm(ALPHA * x + y, g_ref[...], b_ref[...])


def _pool_layer(x, w_in, w_group, scale, w_out, g, b, seq):
    t = x.shape[0]
    tm = ROW_TILE
    hl = POOL_HALO
    assert seq % tm == 0 and tm % hl == 0 and max(POOL_WINDOWS) <= hl
    kern = functools.partial(_pool_kernel, tiles_per_seq=seq // tm)
    ng = len(POOL_WINDOWS)
    return pl.pallas_call(
        kern,
        grid=(t // tm,),
        in_specs=[
            pl.BlockSpec((tm, D_MODEL), lambda i: (i, 0)),
            pl.BlockSpec((hl, D_MODEL), lambda i: (jnp.maximum(i * (tm // hl) - 1, 0), 0)),
            _const_spec((D_MODEL, D_MODEL)),
            _const_spec((ng, POOL_GROUP, POOL_GROUP)),
            _const_spec((1, D_MODEL)),
            _const_spec((D_MODEL, D_MODEL)),
            _const_spec((1, D_MODEL)),
            _const_spec((1, D_MODEL)),
        ],
        out_specs=pl.BlockSpec((tm, D_MODEL), lambda i: (i, 0)),
        out_shape=jax.ShapeDtypeStruct((t, D_MODEL), F32),
        scratch_shapes=[pltpu.VMEM((tm + hl, D_MODEL), F32), pltpu.VMEM((tm, D_MODEL), BF16)],
        compiler_params=_params("parallel"),
        name="pool_layer",
    )(x, x, w_in, w_group, scale, w_out, g, b)


def _log_sigmoid(x):
    return jnp.minimum(x, 0.0) - jnp.log1p(jnp.exp(-jnp.abs(x)))


def _mlstm_kernel(xn_ref, xc_ref, win_ref, wg_ref, wgt_ref, bg_row_ref, bg_col_ref, cw_ref,
                  ng_ref, wout_ref, g_ref, b_ref, o_ref,
                  c_ref, n_ref, m_ref, cprev_ref, cbuf_ref, hn_ref, pa_ref, pb_ref):
    c = pl.program_id(1)

    @pl.when(c == 0)
    def _():
        pb_ref[...] = jnp.zeros_like(pb_ref)

    @pl.when(c <= 1)
    def _():
        c_ref[...] = jnp.zeros_like(c_ref)
        n_ref[...] = jnp.zeros_like(n_ref)
        m_ref[...] = jnp.zeros_like(m_ref)
        cprev_ref[...] = jnp.zeros_like(cprev_ref)

    pl.when(c % 2 == 0)(functools.partial(
        _mlstm_step, pa_ref, pb_ref, xn_ref, xc_ref, win_ref, wg_ref, wgt_ref, bg_row_ref, bg_col_ref, cw_ref,
        ng_ref, wout_ref, g_ref, b_ref, o_ref, c_ref, n_ref, m_ref, cprev_ref, cbuf_ref, hn_ref))
    pl.when(c % 2 == 1)(functools.partial(
        _mlstm_step, pb_ref, pa_ref, xn_ref, xc_ref, win_ref, wg_ref, wgt_ref, bg_row_ref, bg_col_ref, cw_ref,
        ng_ref, wout_ref, g_ref, b_ref, o_ref, c_ref, n_ref, m_ref, cprev_ref, cbuf_ref, hn_ref))


def _mlstm_step(p_write, p_read, xn_ref, xc_ref, win_ref, wg_ref, wgt_ref, bg_row_ref, bg_col_ref, cw_ref,
                ng_ref, wout_ref, g_ref, b_ref, o_ref, c_ref, n_ref, m_ref, cprev_ref, cbuf_ref, hn_ref):
    L = C_CHUNK
    d = D_MODEL
    dh = C_HEAD_DIM
    nh = C_HEADS
    hl = CONV_HALO
    pw = 4 * d // nh

    xnb = xn_ref[...].astype(BF16)
    x = xc_ref[...]
    xb = x.astype(BF16)
    g_col = _dot(xb, wg_ref[...]) + bg_row_ref[...]
    g_row = _dot_nt(wgt_ref[...], xb) + bg_col_ref[...]

    cbuf_ref[0:hl, :] = cprev_ref[...]
    cbuf_ref[hl:hl + L, :] = p_read[:, 0:2 * d]
    cprev_ref[...] = p_read[L - hl:L, 0:2 * d]
    conv = cw_ref[C_CONV - 1:C_CONV, :] * p_read[:, 0:2 * d]
    for j in range(C_CONV - 2, -1, -1):
        off = hl - (C_CONV - 1) + j
        conv = conv + cw_ref[j:j + 1, :] * cbuf_ref[off:off + L, :]
    qk = conv * _sigmoid(conv)

    r = lax.broadcasted_iota(jnp.int32, (L, L), 0)
    cc = lax.broadcasted_iota(jnp.int32, (L, L), 1)
    lower = cc <= r

    for hd in range(nh):
        p_write[:, hd * pw:(hd + 1) * pw] = _dot(xnb, win_ref[:, hd * pw:(hd + 1) * pw])

        i_col = g_col[:, hd:hd + 1]
        i_row = g_row[hd:hd + 1, :]
        lf_col = _log_sigmoid(g_col[:, nh + hd:nh + hd + 1])
        lf_row = _log_sigmoid(g_row[nh + hd:nh + hd + 1, :])
        b_col = jnp.sum(jnp.where(lower, lf_row, 0.0), axis=1, keepdims=True)
        b_row = jnp.sum(jnp.where(r <= cc, lf_col, 0.0), axis=0, keepdims=True)
        b_last = jnp.sum(lf_row, axis=1, keepdims=True)

        q_h = qk[:, hd * dh:(hd + 1) * dh]
        k_h = qk[:, d + hd * dh:d + (hd + 1) * dh] * (dh ** -0.5)
        qb = q_h.astype(BF16)
        kb = k_h.astype(BF16)
        vb = p_read[:, 2 * d + hd * dh:2 * d + (hd + 1) * dh].astype(BF16)
        c_st = c_ref[hd]
        n_st = n_ref[hd]
        m_prev = m_ref[hd]

        d_intra = jnp.where(lower, b_col - b_row + i_row, NEG)
        m_inter = b_col + m_prev
        m_t = jnp.maximum(m_inter, jnp.max(d_intra, axis=1, keepdims=True))
        w = jnp.exp(d_intra - m_t) * _dot_nt(qb, kb)
        s_inter = jnp.exp(m_inter - m_t)
        num = s_inter * _dot(qb, c_st.astype(BF16)) + _dot(w.astype(BF16), vb)
        den = s_inter * jnp.sum(q_h * n_st, axis=1, keepdims=True) + jnp.sum(w, axis=1, keepdims=True)
        ht = num / jnp.maximum(jnp.abs(den), jnp.exp(-m_t))

        gg_col = b_last - b_col + i_col
        gg_row = b_last - b_row + i_row
        m_new = jnp.maximum(b_last + m_prev, jnp.max(gg_row, axis=1, keepdims=True))
        decay = jnp.exp(b_last + m_prev - m_new)
        kw = k_h * jnp.exp(gg_col - m_new)
        c_ref[hd] = decay * c_st + _dot(jnp.transpose(kw).astype(BF16), vb)
        n_ref[hd] = decay * n_st + jnp.sum(kw, axis=0, keepdims=True)
        m_ref[hd] = m_new

        hc = _sigmoid(p_read[:, 3 * d + hd * dh:3 * d + (hd + 1) * dh]) * ht
        mu = jnp.mean(hc, axis=1, keepdims=True)
        hcc = hc - mu
        var = jnp.mean(hcc * hcc, axis=1, keepdims=True)
        hn = hcc * lax.rsqrt(var + LN_EPS) * ng_ref[:, hd * dh:(hd + 1) * dh]
        hn_ref[:, hd * dh:(hd + 1) * dh] = hn.astype(BF16)

    y = _dot(hn_ref[...], wout_ref[...])
    o_ref[...] = _layer_norm(ALPHA * x + y, g_ref[...], b_ref[...])


def _mlstm_layer(x, w_in, b_gates, conv_w, norm_g, w_out, g, b, batch, seq):
    t = x.shape[0]
    L = C_CHUNK
    nc = seq // L
    d = D_MODEL
    nh = C_HEADS
    assert seq % L == 0
    w_gate = w_in[:, 4 * d:]
    wg = jnp.pad(w_gate, ((0, 0), (0, GATE_PAD - 2 * nh)))
    wgt = jnp.pad(w_gate.T, ((0, 16 - 2 * nh), (0, 0)))
    bg_row = jnp.pad(b_gates, (0, GATE_PAD - 2 * nh))[None, :].astype(F32)
    bg_col = jnp.pad(b_gates, (0, 16 - 2 * nh))[:, None].astype(F32)
    return pl.pallas_call(
        _mlstm_kernel,
        grid=(batch, nc + 1),
        in_specs=[
            pl.BlockSpec((L, d), lambda bb, c: (bb * nc + jnp.minimum(c, nc - 1), 0)),
            pl.BlockSpec((L, d), lambda bb, c: (bb * nc + jnp.maximum(c - 1, 0), 0)),
            pl.BlockSpec((d, 4 * d), lambda bb, c: (0, 0), pipeline_mode=pl.Buffered(1)),
            _const_spec((d, GATE_PAD)),
            _const_spec((16, d)),
            _const_spec((1, GATE_PAD)),
            _const_spec((16, 1)),
            _const_spec((C_CONV, 2 * d)),
            _const_spec((1, d)),
            _const_spec((d, d)),
            _const_spec((1, d)),
            _const_spec((1, d)),
        ],
        out_specs=pl.BlockSpec((L, d), lambda bb, c: (bb * nc + jnp.maximum(c - 1, 0), 0)),
        out_shape=jax.ShapeDtypeStruct((t, d), F32),
        scratch_shapes=[
            pltpu.VMEM((nh, C_HEAD_DIM, C_HEAD_DIM), F32),
            pltpu.VMEM((nh, 1, C_HEAD_DIM), F32),
            pltpu.VMEM((nh, 1, 1), F32),
            pltpu.VMEM((CONV_HALO, 2 * d), F32),
            pltpu.VMEM((CONV_HALO + L, 2 * d), F32),
            pltpu.VMEM((L, d), BF16),
            pltpu.VMEM((L, 4 * d), F32),
            pltpu.VMEM((L, 4 * d), F32),
        ],
        compiler_params=_params("arbitrary", "arbitrary"),
        name="mlstm_layer",
    )(x, x, w_in, wg.astype(BF16), wgt.astype(BF16), bg_row, bg_col, conv_w.astype(F32),
      norm_g[None, :].astype(F32), w_out, g, b)


def kernel(x, rel_bias, ln_g, ln_b, ffn_w_gu, ffn_w_down, a_w_in, a_w_out, b_w_in, b_w_group, b_scale, b_w_out,
           c_w_in, c_b_gates, c_conv_w, c_norm_g, c_w_out):
    batch, seq, d = x.shape
    h = x.reshape(batch * seq, d)
    bf = lambda w: w.astype(BF16)
    w_gu_all, w_down_all = bf(ffn_w_gu), bf(ffn_w_down)
    for i in range(DEPTH):
        lg = lambda s: ln_g[i, s][None, :]
        lb = lambda s: ln_b[i, s][None, :]
        h = _ffn(h, w_gu_all, w_down_all, i, 0, lg(0), lb(0))
        kind, j = i % N_MIXERS, i // N_MIXERS
        pre = None
        if kind == 0:
            pre = (_moba_mixer(h, a_w_in[j], rel_bias, batch, seq), bf(a_w_out[j]), lg(1), lb(1))
        elif kind == 1:
            h = _pool_layer(h, bf(b_w_in[j]), bf(b_w_group[j]), b_scale[j][None, :], bf(b_w_out[j]),
                            lg(1), lb(1), seq)
        else:
            h = _mlstm_layer(h, bf(c_w_in[j]), c_b_gates[j], c_conv_w[j], c_norm_g[j], bf(c_w_out[j]),
                             lg(1), lb(1), batch, seq)
        h = _ffn(h, w_gu_all, w_down_all, i, 1, lg(2), lb(2), pre)
    return h.reshape(batch, seq, d)
```

```python
import functools
import math

import jax
import jax.numpy as jnp
from jax import lax
from jax.experimental import pallas as pl
from jax.experimental.pallas import tpu as pltpu

F32 = jnp.float32
BF16 = jnp.bfloat16

D_MODEL = 1024
DEPTH = 4
N_MIXERS = 3
D_FF = 2816
LN_EPS = 1e-5
ALPHA = (2 * DEPTH) ** 0.25
A_HEADS = 8
A_HEAD_DIM = D_MODEL // A_HEADS
MOBA_BLOCK = 256
MOBA_TOPK = 3
REL_BUCKETS = 32
REL_MAX_EXACT = REL_BUCKETS // 2
REL_MAX_DIST = 128
POOL_WINDOWS = (2, 4, 8, 16)
POOL_GROUP = D_MODEL // len(POOL_WINDOWS)
POOL_HALO = 16
C_HEADS = 4
C_HEAD_DIM = D_MODEL // C_HEADS
C_CONV = 4
C_CHUNK = 256
CONV_HALO = 8
GATE_PAD = 128

NEG = -1e30
LOG2E = math.log2(math.e)
MOBA_GROUP = 4
MOBA_HEADS_PER_STEP = 2
MOBA_EXP_ROWS = 64
MOBA_VT_ROWS = 128 + 16
V7X_VMEM_LIMIT = 56 * 1024 * 1024
FFN_CHUNK = 256
FFN_ROW_TILE = 1024
FFN_SUBTILE_ROWS = 256
ROW_TILE = 512

NT_DIMS = (((1,), (1,)), ((), ()))


def _params(*sem):
    return pltpu.CompilerParams(dimension_semantics=sem, vmem_limit_bytes=V7X_VMEM_LIMIT)


def _const_spec(shape):
    nd = len(shape)
    return pl.BlockSpec(shape, lambda *_: (0,) * nd, pipeline_mode=pl.Buffered(1))


def _layer_norm(z, g, b):
    mu = jnp.mean(z, axis=-1, keepdims=True)
    zc = z - mu
    var = jnp.mean(zc * zc, axis=-1, keepdims=True)
    return zc * lax.rsqrt(var + LN_EPS) * g + b


def _sigmoid(x):
    return 1.0 / (1.0 + jnp.exp(-x))


def _dot(a, b):
    return jnp.dot(a, b, preferred_element_type=F32)


def _dot_nt(a, b):
    return lax.dot_general(a, b, NT_DIMS, preferred_element_type=F32)


def _ffn_kernel(*refs, has_pre):
    x_ref = refs[0]
    wgu_ref, wd_ref, g_ref, b_ref, o_ref, xb_ref, h_ref = refs[1 + 4 * has_pre:]
    sm = FFN_SUBTILE_ROWS
    for s in range(x_ref.shape[0] // sm):
        rows = slice(s * sm, (s + 1) * sm)
        slot = s % 2
        x = x_ref[rows, :]
        if has_pre:
            a_ref, wa_ref, ga_ref, ba_ref = refs[1:5]
            x = _layer_norm(ALPHA * x + _dot(a_ref[rows, :], wa_ref[...]), ga_ref[...], ba_ref[...])
        xb_ref[slot] = x.astype(BF16)
        for c in range(D_FF // FFN_CHUNK):
            lo = c * FFN_CHUNK
            xb = xb_ref[slot]
            gate = _dot(xb, wgu_ref[:, lo:lo + FFN_CHUNK])
            up = _dot(xb, wgu_ref[:, D_FF + lo:D_FF + lo + FFN_CHUNK])
            h_ref[slot, :, lo:lo + FFN_CHUNK] = (gate * _sigmoid(gate) * up).astype(BF16)
        y = _dot(h_ref[slot], wd_ref[...])
        o_ref[rows, :] = _layer_norm(ALPHA * x + 0.5 * y, g_ref[...], b_ref[...])


def _ffn(x, w_gu_all, w_down_all, layer, slot, g, b, pre=None):
    t = x.shape[0]
    tm = FFN_ROW_TILE
    sm = FFN_SUBTILE_ROWS
    row = lambda i: (i, 0)
    pick = lambda *_: (layer, slot, 0, 0)
    vec = _const_spec((1, D_MODEL))
    in_specs = [pl.BlockSpec((tm, D_MODEL), row)]
    args = [x]
    if pre is not None:
        in_specs += [pl.BlockSpec((tm, D_MODEL), row), _const_spec((D_MODEL, D_MODEL)), vec, vec]
        args += list(pre)
    in_specs += [
        pl.BlockSpec((None, None, D_MODEL, 2 * D_FF), pick, pipeline_mode=pl.Buffered(1)),
        pl.BlockSpec((None, None, D_FF, D_MODEL), pick, pipeline_mode=pl.Buffered(1)),
        vec, vec,
    ]
    args += [w_gu_all, w_down_all, g, b]
    return pl.pallas_call(
        functools.partial(_ffn_kernel, has_pre=pre is not None),
        grid=(t // tm,),
        in_specs=in_specs,
        out_specs=pl.BlockSpec((tm, D_MODEL), row),
        out_shape=jax.ShapeDtypeStruct((t, D_MODEL), F32),
        scratch_shapes=[pltpu.VMEM((2, sm, D_MODEL), BF16), pltpu.VMEM((2, sm, D_FF), BF16)],
        compiler_params=_params("parallel"),
        name="ffn_pre" if pre is not None else "ffn",
    )(*args)


def _t5_bucket(dist):
    n = jnp.maximum(dist, 0)
    is_small = n < REL_MAX_EXACT
    nf = jnp.maximum(n, 1).astype(F32)
    large = REL_MAX_EXACT + (jnp.log(nf / REL_MAX_EXACT) / math.log(REL_MAX_DIST / REL_MAX_EXACT)
                             * (REL_BUCKETS - REL_MAX_EXACT)).astype(jnp.int32)
    large = jnp.minimum(large, REL_BUCKETS - 1)
    return jnp.where(is_small, n, large)


def _moba_bias_tables(rel_bias):
    blk = MOBA_BLOCK
    nh = rel_bias.shape[1]
    far = rel_bias[REL_BUCKETS - 1][:, None]

    def by_distance(dist):
        onehot = _t5_bucket(dist)[:, None] == jnp.arange(REL_BUCKETS)
        picked = jnp.sum(jnp.where(onehot[None], rel_bias.T[:, None, :], 0.0), axis=-1)
        return (picked - far) * LOG2E

    def toeplitz(v):
        flat = jnp.broadcast_to(v[:, None, :], (nh, blk, 2 * blk)).reshape(nh, 2 * blk * blk)
        skew = flat[:, blk - 1:blk - 1 + blk * (2 * blk - 1)].reshape(nh, blk, 2 * blk - 1)
        return skew[:, :, :blk]

    d = jnp.arange(2 * blk) - (blk - 1)
    own = toeplitz(jnp.where(d[None] >= 0, by_distance(d), NEG))
    adj = toeplitz(by_distance(d + blk))
    return jnp.stack([own, adj, jnp.zeros_like(adj)], axis=1).astype(F32)


def _moba_qkv_kernel(x_ref, wqk_ref, wvt_ref, qk_ref, vt_ref):
    xb = x_ref[...].astype(BF16)
    qk_ref[...] = _dot(xb, wqk_ref[...]).astype(BF16)
    vt = _dot_nt(wvt_ref[...], xb).astype(BF16)
    dh, rows = A_HEAD_DIM, MOBA_VT_ROWS
    for h in range(A_HEADS):
        vt_ref[h * rows:h * rows + dh, :] = vt[h * dh:(h + 1) * dh, :]
        vt_ref[h * rows + dh:(h + 1) * rows, :] = jnp.ones((rows - dh, vt.shape[1]), BF16)


def _moba_qkv(x, w_qk, w_vt, batch, seq):
    t = x.shape[0]
    tm = ROW_TILE
    tps = seq // tm
    return pl.pallas_call(
        _moba_qkv_kernel,
        grid=(t // tm,),
        in_specs=[
            pl.BlockSpec((tm, D_MODEL), lambda i: (i, 0)),
            _const_spec((D_MODEL, 2 * D_MODEL)),
            _const_spec((D_MODEL, D_MODEL)),
        ],
        out_specs=[
            pl.BlockSpec((tm, 2 * D_MODEL), lambda i: (i, 0)),
            pl.BlockSpec((None, A_HEADS * MOBA_VT_ROWS, tm), lambda i: (i // tps, 0, i % tps)),
        ],
        out_shape=[
            jax.ShapeDtypeStruct((t, 2 * D_MODEL), BF16),
            jax.ShapeDtypeStruct((batch, A_HEADS * MOBA_VT_ROWS, seq), BF16),
        ],
        compiler_params=_params("parallel"),
        name="moba_qkv",
    )(x, w_qk, w_vt)


def _moba_kernel(q_ref, k_ref, vt_ref, tab_ref, o_ref, kmean_ref, sel_ref, t_ref, p_ref, m_ref, *, nb):
    blk = MOBA_BLOCK
    dh = A_HEAD_DIM
    gb = MOBA_GROUP
    rows = MOBA_VT_ROWS
    sub = MOBA_EXP_ROWS
    heads = range(MOBA_HEADS_PER_STEP)
    s = pl.program_id(2)

    @pl.when(s == 0)
    def _():
        for hh in heads:
            for j in range(nb):
                kb = k_ref[j * blk:(j + 1) * blk, hh * dh:(hh + 1) * dh].astype(F32)
                kmean_ref[hh, j:j + 1, :] = jnp.mean(kb, axis=0, keepdims=True)
        o_ref[...] = jnp.zeros_like(o_ref)

    def select():
        jidx = lax.broadcasted_iota(jnp.int32, (nb, blk), 0)
        for hh in heads:
            q = q_ref[:, hh * dh:(hh + 1) * dh]
            gate = _dot_nt(kmean_ref[hh].astype(BF16), q)
            cnt = jnp.zeros((nb, blk), F32)
            for jp in range(nb - 1):
                row = gate[jp:jp + 1, :]
                beats = (row > gate) | ((row == gate) & (jp < jidx))
                cnt = cnt + jnp.where(beats & (jp < s), 1.0, 0.0)
            chosen = ((cnt < MOBA_TOPK) & (jidx < s)) | (jidx == s)
            sel_ref[hh] = jnp.where(chosen, 1.0, 0.0)

    def step(ng_score, ng_exp, ng_out):
        first_dynamic = (ng_score - 1) * gb - 1
        if ng_out:
            nk_out = ng_out * gb * blk
            for hh in heads:
                o_aug = _dot(vt_ref[hh * rows:(hh + 1) * rows, 0:nk_out], p_ref[hh, 0:nk_out, :])
                o_t = o_aug[0:dh, :] / o_aug[dh:dh + 1, :]
                o_ref[:, hh * dh:(hh + 1) * dh] = jnp.transpose(o_t).astype(o_ref.dtype)
        if ng_exp:
            m_prev = [m_ref[hh] for hh in heads]
        if ng_score:
            select()
        m8 = [None] * len(heads)
        for j in range(max(ng_score, ng_exp) * gb):
            if j < ng_exp * gb:
                for hh in heads:
                    for r in range(j * blk, (j + 1) * blk, sub):
                        p_ref[hh, r:r + sub, :] = jnp.exp2(t_ref[hh, r:r + sub, :] - m_prev[hh]).astype(BF16)
            if j < ng_score * gb:
                for hh in heads:
                    t = _dot_nt(k_ref[j * blk:(j + 1) * blk, hh * dh:(hh + 1) * dh],
                                q_ref[:, hh * dh:(hh + 1) * dh])
                    if j >= first_dynamic:
                        t = t + tab_ref[hh, jnp.clip(s - j, 0, 2)]
                    t = jnp.where(sel_ref[hh, j:j + 1, :] > 0.5, t, NEG)
                    t_ref[hh, j * blk:(j + 1) * blk, :] = t
                    mb = jnp.max(t.reshape(blk // 8, 8, blk), axis=0)
                    m8[hh] = mb if j == 0 else jnp.maximum(m8[hh], mb)
        if ng_score:
            for hh in heads:
                m_ref[hh] = jnp.max(m8[hh], axis=0, keepdims=True)

    def groups(q):
        return q // gb + 1 if 0 <= q < nb else 0

    def groups_traced(q):
        return jnp.where((q >= 0) & (q < nb), q // gb + 1, 0)

    combos = sorted({(groups(q), groups(q - 1), groups(q - 2)) for q in range(nb + 2)})
    for a, b, c in combos:
        hit = (groups_traced(s) == a) & (groups_traced(s - 1) == b) & (groups_traced(s - 2) == c)
        pl.when(hit)(functools.partial(step, a, b, c))


def _moba_attention(qk, vt, tab, batch, seq):
    blk = MOBA_BLOCK
    nb = seq // blk
    hp = MOBA_HEADS_PER_STEP
    w = hp * A_HEAD_DIM
    ngrp = A_HEADS // hp
    assert seq % blk == 0 and nb % MOBA_GROUP == 0 and A_HEADS % hp == 0
    kern = functools.partial(_moba_kernel, nb=nb)
    return pl.pallas_call(
        kern,
        grid=(batch, ngrp, nb + 2),
        in_specs=[
            pl.BlockSpec((blk, w), lambda b, h, s: (b * nb + jnp.minimum(s, nb - 1), h)),
            pl.BlockSpec((seq, w), lambda b, h, s: (b, ngrp + h)),
            pl.BlockSpec((None, hp * MOBA_VT_ROWS, seq), lambda b, h, s: (b, h, 0)),
            pl.BlockSpec((hp, 3, blk, blk), lambda b, h, s: (h, 0, 0, 0)),
        ],
        out_specs=pl.BlockSpec((blk, w), lambda b, h, s: (b * nb + jnp.maximum(s - 2, 0), h)),
        out_shape=jax.ShapeDtypeStruct((batch * seq, D_MODEL), BF16),
        scratch_shapes=[
            pltpu.VMEM((hp, nb, A_HEAD_DIM), F32),
            pltpu.VMEM((hp, nb, blk), F32),
            pltpu.VMEM((hp, seq, blk), F32),
            pltpu.VMEM((hp, seq, blk), BF16),
            pltpu.VMEM((hp, 1, blk), F32),
        ],
        compiler_params=_params("parallel", "parallel", "arbitrary"),
        name="moba_attn",
    )(qk, qk, vt, tab)


def _moba_mixer(x, w_in, rel_bias, batch, seq):
    assert REL_MAX_DIST <= MOBA_BLOCK
    d = D_MODEL
    c1 = (A_HEAD_DIM ** -0.5) * LOG2E
    w_qk = jnp.concatenate([w_in[:, :d] * c1, w_in[:, d:2 * d]], axis=1).astype(BF16)
    w_vt = w_in[:, 2 * d:].T.astype(BF16)
    qk, vt = _moba_qkv(x, w_qk, w_vt, batch, seq)
    return _moba_attention(qk, vt, _moba_bias_tables(rel_bias), batch, seq)


def _pool_kernel(x_ref, halo_ref, win_ref, wgrp_ref, scale_ref, wout_ref, g_ref, b_ref, o_ref,
                 ubuf_ref, ybuf_ref, *, tiles_per_seq):
    tm = x_ref.shape[0]
    hl = POOL_HALO
    ti = pl.program_id(0) % tiles_per_seq
    x = x_ref[...]
    u_halo = _dot(halo_ref[...].astype(BF16), win_ref[...])
    ubuf_ref[0:hl, :] = jnp.where(ti == 0, 0.0, u_halo)
    ubuf_ref[hl:hl + tm, :] = _dot(x.astype(BF16), win_ref[...])
    pos = ti * tm + lax.broadcasted_iota(jnp.int32, (tm, POOL_GROUP), 0)
    for gi, w in enumerate(POOL_WINDOWS):
        lo = gi * POOL_GROUP
        u = ubuf_ref[hl:hl + tm, lo:lo + POOL_GROUP]
        ws = u
        for d in range(1, w):
            ws = ws + ubuf_ref[hl - d:hl - d + tm, lo:lo + POOL_GROUP]
        cnt = jnp.minimum(pos + 1, w).astype(F32)
        pooled = ws / cnt - u
        yg = _dot(pooled.astype(BF16), wgrp_ref[gi]) * scale_ref[:, lo:lo + POOL_GROUP]
        ybuf_ref[:, lo:lo + POOL_GROUP] = yg.astype(BF16)
    y = _dot(ybuf_ref[...], wout_ref[...])
    o_ref[...] = _layer_norm(ALPHA * x + y, g_ref[...], b_ref[...])


def _pool_layer(x, w_in, w_group, scale, w_out, g, b, seq):
    t = x.shape[0]
    tm = ROW_TILE
    hl = POOL_HALO
    assert seq % tm == 0 and tm % hl == 0 and max(POOL_WINDOWS) <= hl
    kern = functools.partial(_pool_kernel, tiles_per_seq=seq // tm)
    ng = len(POOL_WINDOWS)
    return pl.pallas_call(
        kern,
        grid=(t // tm,),
        in_specs=[
            pl.BlockSpec((tm, D_MODEL), lambda i: (i, 0)),
            pl.BlockSpec((hl, D_MODEL), lambda i: (jnp.maximum(i * (tm // hl) - 1, 0), 0)),
            _const_spec((D_MODEL, D_MODEL)),
            _const_spec((ng, POOL_GROUP, POOL_GROUP)),
            _const_spec((1, D_MODEL)),
            _const_spec((D_MODEL, D_MODEL)),
            _const_spec((1, D_MODEL)),
            _const_spec((1, D_MODEL)),
        ],
        out_specs=pl.BlockSpec((tm, D_MODEL), lambda i: (i, 0)),
        out_shape=jax.ShapeDtypeStruct((t, D_MODEL), F32),
        scratch_shapes=[pltpu.VMEM((tm + hl, D_MODEL), F32), pltpu.VMEM((tm, D_MODEL), BF16)],
        compiler_params=_params("parallel"),
        name="pool_layer",
    )(x, x, w_in, w_group, scale, w_out, g, b)


def _log_sigmoid(x):
    return jnp.minimum(x, 0.0) - jnp.log1p(jnp.exp(-jnp.abs(x)))


def _mlstm_kernel(xn_ref, xc_ref, win_ref, wg_ref, wgt_ref, bg_row_ref, bg_col_ref, cw_ref,
                  ng_ref, wout_ref, g_ref, b_ref, o_ref,
                  c_ref, n_ref, m_ref, cprev_ref, cbuf_ref, hn_ref, pa_ref, pb_ref):
    c = pl.program_id(1)

    @pl.when(c == 0)
    def _():
        pb_ref[...] = jnp.zeros_like(pb_ref)

    @pl.when(c <= 1)
    def _():
        c_ref[...] = jnp.zeros_like(c_ref)
        n_ref[...] = jnp.zeros_like(n_ref)
        m_ref[...] = jnp.zeros_like(m_ref)
        cprev_ref[...] = jnp.zeros_like(cprev_ref)

    pl.when(c % 2 == 0)(functools.partial(
        _mlstm_step, pa_ref, pb_ref, xn_ref, xc_ref, win_ref, wg_ref, wgt_ref, bg_row_ref, bg_col_ref, cw_ref,
        ng_ref, wout_ref, g_ref, b_ref, o_ref, c_ref, n_ref, m_ref, cprev_ref, cbuf_ref, hn_ref))
    pl.when(c % 2 == 1)(functools.partial(
        _mlstm_step, pb_ref, pa_ref, xn_ref, xc_ref, win_ref, wg_ref, wgt_ref, bg_row_ref, bg_col_ref, cw_ref,
        ng_ref, wout_ref, g_ref, b_ref, o_ref, c_ref, n_ref, m_ref, cprev_ref, cbuf_ref, hn_ref))


def _mlstm_step(p_write, p_read, xn_ref, xc_ref, win_ref, wg_ref, wgt_ref, bg_row_ref, bg_col_ref, cw_ref,
                ng_ref, wout_ref, g_ref, b_ref, o_ref, c_ref, n_ref, m_ref, cprev_ref, cbuf_ref, hn_ref):
    L = C_CHUNK
    d = D_MODEL
    dh = C_HEAD_DIM
    nh = C_HEADS
    hl = CONV_HALO
    pw = 4 * d // nh

    xnb = xn_ref[...].astype(BF16)
    x = xc_ref[...]
    xb = x.astype(BF16)
    g_col = _dot(xb, wg_ref[...]) + bg_row_ref[...]
    g_row = _dot_nt(wgt_ref[...], xb) + bg_col_ref[...]

    cbuf_ref[0:hl, :] = cprev_ref[...]
    cbuf_ref[hl:hl + L, :] = p_read[:, 0:2 * d]
    cprev_ref[...] = p_read[L - hl:L, 0:2 * d]
    conv = cw_ref[C_CONV - 1:C_CONV, :] * p_read[:, 0:2 * d]
    for j in range(C_CONV - 2, -1, -1):
        off = hl - (C_CONV - 1) + j
        conv = conv + cw_ref[j:j + 1, :] * cbuf_ref[off:off + L, :]
    qk = conv * _sigmoid(conv)

    r = lax.broadcasted_iota(jnp.int32, (L, L), 0)
    cc = lax.broadcasted_iota(jnp.int32, (L, L), 1)
    lower = cc <= r

    for hd in range(nh):
        p_write[:, hd * pw:(hd + 1) * pw] = _dot(xnb, win_ref[:, hd * pw:(hd + 1) * pw])

        i_col = g_col[:, hd:hd + 1]
        i_row = g_row[hd:hd + 1, :]
        lf_col = _log_sigmoid(g_col[:, nh + hd:nh + hd + 1])
        lf_row = _log_sigmoid(g_row[nh + hd:nh + hd + 1, :])
        b_col = jnp.sum(jnp.where(lower, lf_row, 0.0), axis=1, keepdims=True)
        b_row = jnp.sum(jnp.where(r <= cc, lf_col, 0.0), axis=0, keepdims=True)
        b_last = jnp.sum(lf_row, axis=1, keepdims=True)

        q_h = qk[:, hd * dh:(hd + 1) * dh]
        k_h = qk[:, d + hd * dh:d + (hd + 1) * dh] * (dh ** -0.5)
        qb = q_h.astype(BF16)
        kb = k_h.astype(BF16)
        vb = p_read[:, 2 * d + hd * dh:2 * d + (hd + 1) * dh].astype(BF16)
        c_st = c_ref[hd]
        n_st = n_ref[hd]
        m_prev = m_ref[hd]

        d_intra = jnp.where(lower, b_col - b_row + i_row, NEG)
        m_inter = b_col + m_prev
        m_t = jnp.maximum(m_inter, jnp.max(d_intra, axis=1, keepdims=True))
        w = jnp.exp(d_intra - m_t) * _dot_nt(qb, kb)
        s_inter = jnp.exp(m_inter - m_t)
        num = s_inter * _dot(qb, c_st.astype(BF16)) + _dot(w.astype(BF16), vb)
        den = s_inter * jnp.sum(q_h * n_st, axis=1, keepdims=True) + jnp.sum(w, axis=1, keepdims=True)
        ht = num / jnp.maximum(jnp.abs(den), jnp.exp(-m_t))

        gg_col = b_last - b_col + i_col
        gg_row = b_last - b_row + i_row
        m_new = jnp.maximum(b_last + m_prev, jnp.max(gg_row, axis=1, keepdims=True))
        decay = jnp.exp(b_last + m_prev - m_new)
        kw = k_h * jnp.exp(gg_col - m_new)
        c_ref[hd] = decay * c_st + _dot(jnp.transpose(kw).astype(BF16), vb)
        n_ref[hd] = decay * n_st + jnp.sum(kw, axis=0, keepdims=True)
        m_ref[hd] = m_new

        hc = _sigmoid(p_read[:, 3 * d + hd * dh:3 * d + (hd + 1) * dh]) * ht
        mu = jnp.mean(hc, axis=1, keepdims=True)
        hcc = hc - mu
        var = jnp.mean(hcc * hcc, axis=1, keepdims=True)
        hn = hcc * lax.rsqrt(var + LN_EPS) * ng_ref[:, hd * dh:(hd + 1) * dh]
        hn_ref[:, hd * dh:(hd + 1) * dh] = hn.astype(BF16)

    y = _dot(hn_ref[...], wout_ref[...])
    o_ref[...] = _layer_norm(ALPHA * x + y, g_ref[...], b_ref[...])


def _mlstm_layer(x, w_in, b_gates, conv_w, norm_g, w_out, g, b, batch, seq):
    t = x.shape[0]
    L = C_CHUNK
    nc = seq // L
    d = D_MODEL
    nh = C_HEADS
    assert seq % L == 0
    w_gate = w_in[:, 4 * d:]
    wg = jnp.pad(w_gate, ((0, 0), (0, GATE_PAD - 2 * nh)))
    wgt = jnp.pad(w_gate.T, ((0, 16 - 2 * nh), (0, 0)))
    bg_row = jnp.pad(b_gates, (0, GATE_PAD - 2 * nh))[None, :].astype(F32)
    bg_col = jnp.pad(b_gates, (0, 16 - 2 * nh))[:, None].astype(F32)
    return pl.pallas_call(
        _mlstm_kernel,
        grid=(batch, nc + 1),
        in_specs=[
            pl.BlockSpec((L, d), lambda bb, c: (bb * nc + jnp.minimum(c, nc - 1), 0)),
            pl.BlockSpec((L, d), lambda bb, c: (bb * nc + jnp.maximum(c - 1, 0), 0)),
            pl.BlockSpec((d, 4 * d), lambda bb, c: (0, 0), pipeline_mode=pl.Buffered(1)),
            _const_spec((d, GATE_PAD)),
            _const_spec((16, d)),
            _const_spec((1, GATE_PAD)),
            _const_spec((16, 1)),
            _const_spec((C_CONV, 2 * d)),
            _const_spec((1, d)),
            _const_spec((d, d)),
            _const_spec((1, d)),
            _const_spec((1, d)),
        ],
        out_specs=pl.BlockSpec((L, d), lambda bb, c: (bb * nc + jnp.maximum(c - 1, 0), 0)),
        out_shape=jax.ShapeDtypeStruct((t, d), F32),
        scratch_shapes=[
            pltpu.VMEM((nh, C_HEAD_DIM, C_HEAD_DIM), F32),
            pltpu.VMEM((nh, 1, C_HEAD_DIM), F32),
            pltpu.VMEM((nh, 1, 1), F32),
            pltpu.VMEM((CONV_HALO, 2 * d), F32),
            pltpu.VMEM((CONV_HALO + L, 2 * d), F32),
            pltpu.VMEM((L, d), BF16),
            pltpu.VMEM((L, 4 * d), F32),
            pltpu.VMEM((L, 4 * d), F32),
        ],
        compiler_params=_params("arbitrary", "arbitrary"),
        name="mlstm_layer",
    )(x, x, w_in, wg.astype(BF16), wgt.astype(BF16), bg_row, bg_col, conv_w.astype(F32),
      norm_g[None, :].astype(F32), w_out, g, b)


def kernel(x, rel_bias, ln_g, ln_b, ffn_w_gu, ffn_w_down, a_w_in, a_w_out, b_w_in, b_w_group, b_scale, b_w_out,
           c_w_in, c_b_gates, c_conv_w, c_norm_g, c_w_out):
    batch, seq, d = x.shape
    h = x.reshape(batch * seq, d)
    bf = lambda w: w.astype(BF16)
    w_gu_all, w_down_all = bf(ffn_w_gu), bf(ffn_w_down)
    for i in range(DEPTH):
        lg = lambda s: ln_g[i, s][None, :]
        lb = lambda s: ln_b[i, s][None, :]
        h = _ffn(h, w_gu_all, w_down_all, i, 0, lg(0), lb(0))
        kind, j = i % N_MIXERS, i // N_MIXERS
        pre = None
        if kind == 0:
            pre = (_moba_mixer(h, a_w_in[j], rel_bias, batch, seq), bf(a_w_out[j]), lg(1), lb(1))
        elif kind == 1:
            h = _pool_layer(h, bf(b_w_in[j]), bf(b_w_group[j]), b_scale[j][None, :], bf(b_w_out[j]),
                            lg(1), lb(1), seq)
        else:
            h = _mlstm_layer(h, bf(c_w_in[j]), c_b_gates[j], c_conv_w[j], c_norm_g[j], bf(c_w_out[j]),
                             lg(1), lb(1), batch, seq)
        h = _ffn(h, w_gu_all, w_down_all, i, 1, lg(2), lb(2), pre)
    return h.reshape(batch, seq, d)
```

```python
import functools
import math

import jax
import jax.numpy as jnp
from jax import lax
from jax.experimental import pallas as pl
from jax.experimental.pallas import tpu as pltpu

F32 = jnp.float32
BF16 = jnp.bfloat16

D_MODEL = 1024
DEPTH = 4
N_MIXERS = 3
D_FF = 2816
LN_EPS = 1e-5
ALPHA = (2 * DEPTH) ** 0.25
A_HEADS = 8
A_HEAD_DIM = D_MODEL // A_HEADS
MOBA_BLOCK = 256
MOBA_TOPK = 3
REL_BUCKETS = 32
REL_MAX_EXACT = REL_BUCKETS // 2
REL_MAX_DIST = 128
POOL_WINDOWS = (2, 4, 8, 16)
POOL_GROUP = D_MODEL // len(POOL_WINDOWS)
POOL_HALO = 16
C_HEADS = 4
C_HEAD_DIM = D_MODEL // C_HEADS
C_CONV = 4
C_CHUNK = 256
CONV_HALO = 8
GATE_PAD = 128

NEG = -1e30
LOG2E = math.log2(math.e)
MOBA_GROUP = 4
MOBA_HEADS_PER_STEP = 2
MOBA_EXP_ROWS = 64
MOBA_VT_ROWS = 128 + 16
V7X_VMEM_LIMIT = 56 * 1024 * 1024
FFN_CHUNK = 256
FFN_ROW_TILE = 1024
FFN_SUBTILE_ROWS = 256
FFN_STAGE_COLS = 512
FFN_STAGE_ROWS = 256
ROW_TILE = 512

NT_DIMS = (((1,), (1,)), ((), ()))


def _params(*sem):
    return pltpu.CompilerParams(dimension_semantics=sem, vmem_limit_bytes=V7X_VMEM_LIMIT)


def _const_spec(shape):
    nd = len(shape)
    return pl.BlockSpec(shape, lambda *_: (0,) * nd, pipeline_mode=pl.Buffered(1))


def _layer_norm(z, g, b):
    mu = jnp.mean(z, axis=-1, keepdims=True)
    zc = z - mu
    var = jnp.mean(zc * zc, axis=-1, keepdims=True)
    return zc * lax.rsqrt(var + LN_EPS) * g + b


def _sigmoid(x):
    return 1.0 / (1.0 + jnp.exp(-x))


def _dot(a, b):
    return jnp.dot(a, b, preferred_element_type=F32)


def _dot_nt(a, b):
    return lax.dot_general(a, b, NT_DIMS, preferred_element_type=F32)


def _stage_as_bf16(chunks, stage_ref, sem_ref):
    def copy(c):
        return pltpu.make_async_copy(chunks[c][0], stage_ref.at[c % 2], sem_ref.at[c % 2])

    for c in range(min(2, len(chunks))):
        copy(c).start()
    for c in range(len(chunks)):
        copy(c).wait()
        dst = chunks[c][1]
        dst[...] = stage_ref[c % 2].astype(BF16)
        if c + 2 < len(chunks):
            copy(c + 2).start()


def _ffn_kernel(*refs, has_pre, layer, slot):
    x_ref = refs[0]
    (wgu_hbm, wd_hbm, g_ref, b_ref, o_ref,
     xb_ref, h_ref, wgu_ref, wd_ref, stage_gu_ref, stage_d_ref, sem_gu_ref, sem_d_ref) = refs[1 + 4 * has_pre:]

    @pl.when(pl.program_id(0) == 0)
    def _():
        wc, wr = FFN_STAGE_COLS, FFN_STAGE_ROWS
        gu = wgu_hbm.at[layer, slot]
        dn = wd_hbm.at[layer, slot]
        _stage_as_bf16([(gu.at[:, pl.ds(c * wc, wc)], wgu_ref.at[:, pl.ds(c * wc, wc)])
                        for c in range(2 * D_FF // wc)], stage_gu_ref, sem_gu_ref)
        _stage_as_bf16([(dn.at[pl.ds(r * wr, wr), :], wd_ref.at[pl.ds(r * wr, wr), :])
                        for r in range(D_FF // wr)], stage_d_ref, sem_d_ref)

    sm = FFN_SUBTILE_ROWS
    for s in range(x_ref.shape[0] // sm):
        rows = slice(s * sm, (s + 1) * sm)
        buf = s % 2
        x = x_ref[rows, :]
        if has_pre:
            a_ref, wa_ref, ga_ref, ba_ref = refs[1:5]
            x = _layer_norm(ALPHA * x + _dot(a_ref[rows, :], wa_ref[...]), ga_ref[...], ba_ref[...])
        xb_ref[buf] = x.astype(BF16)
        for c in range(D_FF // FFN_CHUNK):
            lo = c * FFN_CHUNK
            xb = xb_ref[buf]
            gate = _dot(xb, wgu_ref[:, lo:lo + FFN_CHUNK])
            up = _dot(xb, wgu_ref[:, D_FF + lo:D_FF + lo + FFN_CHUNK])
            h_ref[buf, :, lo:lo + FFN_CHUNK] = (gate * _sigmoid(gate) * up).astype(BF16)
        y = _dot(h_ref[buf], wd_ref[...])
        o_ref[rows, :] = _layer_norm(ALPHA * x + 0.5 * y, g_ref[...], b_ref[...])


def _ffn(x, w_gu_all, w_down_all, layer, slot, g, b, pre=None):
    t = x.shape[0]
    tm = FFN_ROW_TILE
    sm = FFN_SUBTILE_ROWS
    assert (2 * D_FF) % FFN_STAGE_COLS == 0 and D_FF % FFN_STAGE_ROWS == 0
    row = lambda i: (i, 0)
    vec = _const_spec((1, D_MODEL))
    hbm = pl.BlockSpec(memory_space=pl.ANY)
    in_specs = [pl.BlockSpec((tm, D_MODEL), row)]
    args = [x]
    if pre is not None:
        in_specs += [pl.BlockSpec((tm, D_MODEL), row), _const_spec((D_MODEL, D_MODEL)), vec, vec]
        args += list(pre)
    in_specs += [hbm, hbm, vec, vec]
    args += [w_gu_all, w_down_all, g, b]
    return pl.pallas_call(
        functools.partial(_ffn_kernel, has_pre=pre is not None, layer=layer, slot=slot),
        grid=(t // tm,),
        in_specs=in_specs,
        out_specs=pl.BlockSpec((tm, D_MODEL), row),
        out_shape=jax.ShapeDtypeStruct((t, D_MODEL), F32),
        scratch_shapes=[
            pltpu.VMEM((2, sm, D_MODEL), BF16),
            pltpu.VMEM((2, sm, D_FF), BF16),
            pltpu.VMEM((D_MODEL, 2 * D_FF), BF16),
            pltpu.VMEM((D_FF, D_MODEL), BF16),
            pltpu.VMEM((2, D_MODEL, FFN_STAGE_COLS), F32),
            pltpu.VMEM((2, FFN_STAGE_ROWS, D_MODEL), F32),
            pltpu.SemaphoreType.DMA((2,)),
            pltpu.SemaphoreType.DMA((2,)),
        ],
        compiler_params=_params("arbitrary"),
        name="ffn_pre" if pre is not None else "ffn",
    )(*args)


def _t5_bucket(dist):
    n = jnp.maximum(dist, 0)
    is_small = n < REL_MAX_EXACT
    nf = jnp.maximum(n, 1).astype(F32)
    large = REL_MAX_EXACT + (jnp.log(nf / REL_MAX_EXACT) / math.log(REL_MAX_DIST / REL_MAX_EXACT)
                             * (REL_BUCKETS - REL_MAX_EXACT)).astype(jnp.int32)
    large = jnp.minimum(large, REL_BUCKETS - 1)
    return jnp.where(is_small, n, large)


def _moba_bias_tables(rel_bias):
    blk = MOBA_BLOCK
    nh = rel_bias.shape[1]
    far = rel_bias[REL_BUCKETS - 1][:, None]

    def by_distance(dist):
        onehot = _t5_bucket(dist)[:, None] == jnp.arange(REL_BUCKETS)
        picked = jnp.sum(jnp.where(onehot[None], rel_bias.T[:, None, :], 0.0), axis=-1)
        return (picked - far) * LOG2E

    def toeplitz(v):
        flat = jnp.broadcast_to(v[:, None, :], (nh, blk, 2 * blk)).reshape(nh, 2 * blk * blk)
        skew = flat[:, blk - 1:blk - 1 + blk * (2 * blk - 1)].reshape(nh, blk, 2 * blk - 1)
        return skew[:, :, :blk]

    d = jnp.arange(2 * blk) - (blk - 1)
    own = toeplitz(jnp.where(d[None] >= 0, by_distance(d), NEG))
    adj = toeplitz(by_distance(d + blk))
    return jnp.stack([own, adj, jnp.zeros_like(adj)], axis=1).astype(F32)


def _moba_qkv_kernel(x_ref, wqk_ref, wvt_ref, qk_ref, vt_ref):
    xb = x_ref[...].astype(BF16)
    qk_ref[...] = _dot(xb, wqk_ref[...]).astype(BF16)
    vt = _dot_nt(wvt_ref[...], xb).astype(BF16)
    dh, rows = A_HEAD_DIM, MOBA_VT_ROWS
    for h in range(A_HEADS):
        vt_ref[h * rows:h * rows + dh, :] = vt[h * dh:(h + 1) * dh, :]
        vt_ref[h * rows + dh:(h + 1) * rows, :] = jnp.ones((rows - dh, vt.shape[1]), BF16)


def _moba_qkv(x, w_qk, w_vt, batch, seq):
    t = x.shape[0]
    tm = ROW_TILE
    tps = seq // tm
    return pl.pallas_call(
        _moba_qkv_kernel,
        grid=(t // tm,),
        in_specs=[
            pl.BlockSpec((tm, D_MODEL), lambda i: (i, 0)),
            _const_spec((D_MODEL, 2 * D_MODEL)),
            _const_spec((D_MODEL, D_MODEL)),
        ],
        out_specs=[
            pl.BlockSpec((tm, 2 * D_MODEL), lambda i: (i, 0)),
            pl.BlockSpec((None, A_HEADS * MOBA_VT_ROWS, tm), lambda i: (i // tps, 0, i % tps)),
        ],
        out_shape=[
            jax.ShapeDtypeStruct((t, 2 * D_MODEL), BF16),
            jax.ShapeDtypeStruct((batch, A_HEADS * MOBA_VT_ROWS, seq), BF16),
        ],
        compiler_params=_params("parallel"),
        name="moba_qkv",
    )(x, w_qk, w_vt)


def _moba_kernel(q_ref, k_ref, vt_ref, tab_ref, o_ref, kmean_ref, sel_ref, t_ref, p_ref, m_ref, *, nb):
    blk = MOBA_BLOCK
    dh = A_HEAD_DIM
    gb = MOBA_GROUP
    rows = MOBA_VT_ROWS
    sub = MOBA_EXP_ROWS
    heads = range(MOBA_HEADS_PER_STEP)
    s = pl.program_id(2)

    @pl.when(s == 0)
    def _():
        for hh in heads:
            for j in range(nb):
                kb = k_ref[j * blk:(j + 1) * blk, hh * dh:(hh + 1) * dh].astype(F32)
                kmean_ref[hh, j:j + 1, :] = jnp.mean(kb, axis=0, keepdims=True)
        o_ref[...] = jnp.zeros_like(o_ref)

    def select():
        jidx = lax.broadcasted_iota(jnp.int32, (nb, blk), 0)
        for hh in heads:
            q = q_ref[:, hh * dh:(hh + 1) * dh]
            gate = _dot_nt(kmean_ref[hh].astype(BF16), q)
            cnt = jnp.zeros((nb, blk), F32)
            for jp in range(nb - 1):
                row = gate[jp:jp + 1, :]
                beats = (row > gate) | ((row == gate) & (jp < jidx))
                cnt = cnt + jnp.where(beats & (jp < s), 1.0, 0.0)
            chosen = ((cnt < MOBA_TOPK) & (jidx < s)) | (jidx == s)
            sel_ref[hh] = jnp.where(chosen, 1.0, 0.0)

    def step(ng_score, ng_finish):
        first_dynamic = (ng_score - 1) * gb - 1
        nk_finish = ng_finish * gb * blk
        if ng_finish:
            m_prev = [m_ref[hh] for hh in heads]
        if ng_score:
            select()
        m8 = [None] * len(heads)
        for j in range(max(ng_score, ng_finish) * gb):
            if j < ng_finish * gb:
                for hh in heads:
                    for r in range(j * blk, (j + 1) * blk, sub):
                        p_ref[hh, r:r + sub, :] = jnp.exp2(t_ref[hh, r:r + sub, :] - m_prev[hh]).astype(BF16)
            if j < ng_score * gb:
                for hh in heads:
                    t = _dot_nt(k_ref[j * blk:(j + 1) * blk, hh * dh:(hh + 1) * dh],
                                q_ref[:, hh * dh:(hh + 1) * dh])
                    if j >= first_dynamic:
                        t = t + tab_ref[hh, jnp.clip(s - j, 0, 2)]
                    t = jnp.where(sel_ref[hh, j:j + 1, :] > 0.5, t, NEG)
                    t_ref[hh, j * blk:(j + 1) * blk, :] = t
                    mb = jnp.max(t.reshape(blk // 8, 8, blk), axis=0)
                    m8[hh] = mb if j == 0 else jnp.maximum(m8[hh], mb)
        if ng_score:
            for hh in heads:
                m_ref[hh] = jnp.max(m8[hh], axis=0, keepdims=True)
        if ng_finish:
            for hh in heads:
                o_aug = _dot(vt_ref[hh * rows:(hh + 1) * rows, 0:nk_finish], p_ref[hh, 0:nk_finish, :])
                o_t = o_aug[0:dh, :] / o_aug[dh:dh + 1, :]
                o_ref[:, hh * dh:(hh + 1) * dh] = jnp.transpose(o_t).astype(o_ref.dtype)

    ng_score = jnp.where(s < nb, s // gb + 1, 0)
    ng_finish = jnp.where(s >= 1, (s - 1) // gb + 1, 0)
    combos = sorted({(q // gb + 1 if q < nb else 0, (q - 1) // gb + 1 if q >= 1 else 0) for q in range(nb + 1)})
    for a, b in combos:
        pl.when((ng_score == a) & (ng_finish == b))(functools.partial(step, a, b))


def _moba_attention(qk, vt, tab, batch, seq):
    blk = MOBA_BLOCK
    nb = seq // blk
    hp = MOBA_HEADS_PER_STEP
    w = hp * A_HEAD_DIM
    ngrp = A_HEADS // hp
    assert seq % blk == 0 and nb % MOBA_GROUP == 0 and A_HEADS % hp == 0
    kern = functools.partial(_moba_kernel, nb=nb)
    return pl.pallas_call(
        kern,
        grid=(batch, ngrp, nb + 1),
        in_specs=[
            pl.BlockSpec((blk, w), lambda b, h, s: (b * nb + jnp.minimum(s, nb - 1), h)),
            pl.BlockSpec((seq, w), lambda b, h, s: (b, ngrp + h)),
            pl.BlockSpec((None, hp * MOBA_VT_ROWS, seq), lambda b, h, s: (b, h, 0)),
            pl.BlockSpec((hp, 3, blk, blk), lambda b, h, s: (h, 0, 0, 0)),
        ],
        out_specs=pl.BlockSpec((blk, w), lambda b, h, s: (b * nb + jnp.maximum(s - 1, 0), h)),
        out_shape=jax.ShapeDtypeStruct((batch * seq, D_MODEL), BF16),
        scratch_shapes=[
            pltpu.VMEM((hp, nb, A_HEAD_DIM), F32),
            pltpu.VMEM((hp, nb, blk), F32),
            pltpu.VMEM((hp, seq, blk), F32),
            pltpu.VMEM((hp, seq, blk), BF16),
            pltpu.VMEM((hp, 1, blk), F32),
        ],
        compiler_params=_params("parallel", "parallel", "arbitrary"),
        name="moba_attn",
    )(qk, qk, vt, tab)


def _moba_mixer(x, w_in, rel_bias, batch, seq):
    assert REL_MAX_DIST <= MOBA_BLOCK
    d = D_MODEL
    c1 = (A_HEAD_DIM ** -0.5) * LOG2E
    w_qk = jnp.concatenate([w_in[:, :d] * c1, w_in[:, d:2 * d]], axis=1).astype(BF16)
    w_vt = w_in[:, 2 * d:].T.astype(BF16)
    qk, vt = _moba_qkv(x, w_qk, w_vt, batch, seq)
    return _moba_attention(qk, vt, _moba_bias_tables(rel_bias), batch, seq)


def _pool_kernel(x_ref, halo_ref, win_ref, wgrp_ref, scale_ref, wout_ref, g_ref, b_ref, o_ref,
                 ubuf_ref, ybuf_ref, *, tiles_per_seq):
    tm = x_ref.shape[0]
    hl = POOL_HALO
    ti = pl.program_id(0) % tiles_per_seq
    x = x_ref[...]
    u_halo = _dot(halo_ref[...].astype(BF16), win_ref[...])
    ubuf_ref[0:hl, :] = jnp.where(ti == 0, 0.0, u_halo)
    ubuf_ref[hl:hl + tm, :] = _dot(x.astype(BF16), win_ref[...])
    pos = ti * tm + lax.broadcasted_iota(jnp.int32, (tm, POOL_GROUP), 0)
    for gi, w in enumerate(POOL_WINDOWS):
        lo = gi * POOL_GROUP
        u = ubuf_ref[hl:hl + tm, lo:lo + POOL_GROUP]
        ws = u
        for d in range(1, w):
            ws = ws + ubuf_ref[hl - d:hl - d + tm, lo:lo + POOL_GROUP]
        cnt = jnp.minimum(pos + 1, w).astype(F32)
        pooled = ws / cnt - u
        yg = _dot(pooled.astype(BF16), wgrp_ref[gi]) * scale_ref[:, lo:lo + POOL_GROUP]
        ybuf_ref[:, lo:lo + POOL_GROUP] = yg.astype(BF16)
    y = _dot(ybuf_ref[...], wout_ref[...])
    o_ref[...] = _layer_norm(ALPHA * x + y, g_ref[...], b_ref[...])


def _pool_layer(x, w_in, w_group, scale, w_out, g, b, seq):
    t = x.shape[0]
    tm = ROW_TILE
    hl = POOL_HALO
    assert seq % tm == 0 and tm % hl == 0 and max(POOL_WINDOWS) <= hl
    kern = functools.partial(_pool_kernel, tiles_per_seq=seq // tm)
    ng = len(POOL_WINDOWS)
    return pl.pallas_call(
        kern,
        grid=(t // tm,),
        in_specs=[
            pl.BlockSpec((tm, D_MODEL), lambda i: (i, 0)),
            pl.BlockSpec((hl, D_MODEL), lambda i: (jnp.maximum(i * (tm // hl) - 1, 0), 0)),
            _const_spec((D_MODEL, D_MODEL)),
            _const_spec((ng, POOL_GROUP, POOL_GROUP)),
            _const_spec((1, D_MODEL)),
            _const_spec((D_MODEL, D_MODEL)),
            _const_spec((1, D_MODEL)),
            _const_spec((1, D_MODEL)),
        ],
        out_specs=pl.BlockSpec((tm, D_MODEL), lambda i: (i, 0)),
        out_shape=jax.ShapeDtypeStruct((t, D_MODEL), F32),
        scratch_shapes=[pltpu.VMEM((tm + hl, D_MODEL), F32), pltpu.VMEM((tm, D_MODEL), BF16)],
        compiler_params=_params("parallel"),
        name="pool_layer",
    )(x, x, w_in, w_group, scale, w_out, g, b)


def _log_sigmoid(x):
    return jnp.minimum(x, 0.0) - jnp.log1p(jnp.exp(-jnp.abs(x)))


def _mlstm_kernel(xn_ref, xc_ref, win_ref, wg_ref, wgt_ref, bg_row_ref, bg_col_ref, cw_ref,
                  ng_ref, wout_ref, g_ref, b_ref, o_ref,
                  c_ref, n_ref, m_ref, cprev_ref, cbuf_ref, hn_ref, pa_ref, pb_ref):
    c = pl.program_id(1)

    @pl.when(c == 0)
    def _():
        pb_ref[...] = jnp.zeros_like(pb_ref)

    @pl.when(c <= 1)
    def _():
        c_ref[...] = jnp.zeros_like(c_ref)
        n_ref[...] = jnp.zeros_like(n_ref)
        m_ref[...] = jnp.zeros_like(m_ref)
        cprev_ref[...] = jnp.zeros_like(cprev_ref)

    pl.when(c % 2 == 0)(functools.partial(
        _mlstm_step, pa_ref, pb_ref, xn_ref, xc_ref, win_ref, wg_ref, wgt_ref, bg_row_ref, bg_col_ref, cw_ref,
        ng_ref, wout_ref, g_ref, b_ref, o_ref, c_ref, n_ref, m_ref, cprev_ref, cbuf_ref, hn_ref))
    pl.when(c % 2 == 1)(functools.partial(
        _mlstm_step, pb_ref, pa_ref, xn_ref, xc_ref, win_ref, wg_ref, wgt_ref, bg_row_ref, bg_col_ref, cw_ref,
        ng_ref, wout_ref, g_ref, b_ref, o_ref, c_ref, n_ref, m_ref, cprev_ref, cbuf_ref, hn_ref))


def _mlstm_step(p_write, p_read, xn_ref, xc_ref, win_ref, wg_ref, wgt_ref, bg_row_ref, bg_col_ref, cw_ref,
                ng_ref, wout_ref, g_ref, b_ref, o_ref, c_ref, n_ref, m_ref, cprev_ref, cbuf_ref, hn_ref):
    L = C_CHUNK
    d = D_MODEL
    dh = C_HEAD_DIM
    nh = C_HEADS
    hl = CONV_HALO
    pw = 4 * d // nh

    xnb = xn_ref[...].astype(BF16)
    x = xc_ref[...]
    xb = x.astype(BF16)
    g_col = _dot(xb, wg_ref[...]) + bg_row_ref[...]
    g_row = _dot_nt(wgt_ref[...], xb) + bg_col_ref[...]

    cbuf_ref[0:hl, :] = cprev_ref[...]
    cbuf_ref[hl:hl + L, :] = p_read[:, 0:2 * d]
    cprev_ref[...] = p_read[L - hl:L, 0:2 * d]
    conv = cw_ref[C_CONV - 1:C_CONV, :] * p_read[:, 0:2 * d]
    for j in range(C_CONV - 2, -1, -1):
        off = hl - (C_CONV - 1) + j
        conv = conv + cw_ref[j:j + 1, :] * cbuf_ref[off:off + L, :]
    qk = conv * _sigmoid(conv)

    r = lax.broadcasted_iota(jnp.int32, (L, L), 0)
    cc = lax.broadcasted_iota(jnp.int32, (L, L), 1)
    lower = cc <= r

    for hd in range(nh):
        p_write[:, hd * pw:(hd + 1) * pw] = _dot(xnb, win_ref[:, hd * pw:(hd + 1) * pw])

        i_col = g_col[:, hd:hd + 1]
        i_row = g_row[hd:hd + 1, :]
        lf_col = _log_sigmoid(g_col[:, nh + hd:nh + hd + 1])
        lf_row = _log_sigmoid(g_row[nh + hd:nh + hd + 1, :])
        b_col = jnp.sum(jnp.where(lower, lf_row, 0.0), axis=1, keepdims=True)
        b_row = jnp.sum(jnp.where(r <= cc, lf_col, 0.0), axis=0, keepdims=True)
        b_last = jnp.sum(lf_row, axis=1, keepdims=True)

        q_h = qk[:, hd * dh:(hd + 1) * dh]
        k_h = qk[:, d + hd * dh:d + (hd + 1) * dh] * (dh ** -0.5)
        qb = q_h.astype(BF16)
        kb = k_h.astype(BF16)
        vb = p_read[:, 2 * d + hd * dh:2 * d + (hd + 1) * dh].astype(BF16)
        c_st = c_ref[hd]
        n_st = n_ref[hd]
        m_prev = m_ref[hd]

        d_intra = jnp.where(lower, b_col - b_row + i_row, NEG)
        m_inter = b_col + m_prev
        m_t = jnp.maximum(m_inter, jnp.max(d_intra, axis=1, keepdims=True))
        w = jnp.exp(d_intra - m_t) * _dot_nt(qb, kb)
        s_inter = jnp.exp(m_inter - m_t)
        num = s_inter * _dot(qb, c_st.astype(BF16)) + _dot(w.astype(BF16), vb)
        den = s_inter * jnp.sum(q_h * n_st, axis=1, keepdims=True) + jnp.sum(w, axis=1, keepdims=True)
        ht = num / jnp.maximum(jnp.abs(den), jnp.exp(-m_t))

        gg_col = b_last - b_col + i_col
        gg_row = b_last - b_row + i_row
        m_new = jnp.maximum(b_last + m_prev, jnp.max(gg_row, axis=1, keepdims=True))
        decay = jnp.exp(b_last + m_prev - m_new)
        kw = k_h * jnp.exp(gg_col - m_new)
        c_ref[hd] = decay * c_st + _dot(jnp.transpose(kw).astype(BF16), vb)
        n_ref[hd] = decay * n_st + jnp.sum(kw, axis=0, keepdims=True)
        m_ref[hd] = m_new

        hc = _sigmoid(p_read[:, 3 * d + hd * dh:3 * d + (hd + 1) * dh]) * ht
        mu = jnp.mean(hc, axis=1, keepdims=True)
        hcc = hc - mu
        var = jnp.mean(hcc * hcc, axis=1, keepdims=True)
        hn = hcc * lax.rsqrt(var + LN_EPS) * ng_ref[:, hd * dh:(hd + 1) * dh]
        hn_ref[:, hd * dh:(hd + 1) * dh] = hn.astype(BF16)

    y = _dot(hn_ref[...], wout_ref[...])
    o_ref[...] = _layer_norm(ALPHA * x + y, g_ref[...], b_ref[...])


def _mlstm_layer(x, w_in, b_gates, conv_w, norm_g, w_out, g, b, batch, seq):
    t = x.shape[0]
    L = C_CHUNK
    nc = seq // L
    d = D_MODEL
    nh = C_HEADS
    assert seq % L == 0
    w_gate = w_in[:, 4 * d:]
    wg = jnp.pad(w_gate, ((0, 0), (0, GATE_PAD - 2 * nh)))
    wgt = jnp.pad(w_gate.T, ((0, 16 - 2 * nh), (0, 0)))
    bg_row = jnp.pad(b_gates, (0, GATE_PAD - 2 * nh))[None, :].astype(F32)
    bg_col = jnp.pad(b_gates, (0, 16 - 2 * nh))[:, None].astype(F32)
    return pl.pallas_call(
        _mlstm_kernel,
        grid=(batch, nc + 1),
        in_specs=[
            pl.BlockSpec((L, d), lambda bb, c: (bb * nc + jnp.minimum(c, nc - 1), 0)),
            pl.BlockSpec((L, d), lambda bb, c: (bb * nc + jnp.maximum(c - 1, 0), 0)),
            pl.BlockSpec((d, 4 * d), lambda bb, c: (0, 0), pipeline_mode=pl.Buffered(1)),
            _const_spec((d, GATE_PAD)),
            _const_spec((16, d)),
            _const_spec((1, GATE_PAD)),
            _const_spec((16, 1)),
            _const_spec((C_CONV, 2 * d)),
            _const_spec((1, d)),
            _const_spec((d, d)),
            _const_spec((1, d)),
            _const_spec((1, d)),
        ],
        out_specs=pl.BlockSpec((L, d), lambda bb, c: (bb * nc + jnp.maximum(c - 1, 0), 0)),
        out_shape=jax.ShapeDtypeStruct((t, d), F32),
        scratch_shapes=[
            pltpu.VMEM((nh, C_HEAD_DIM, C_HEAD_DIM), F32),
            pltpu.VMEM((nh, 1, C_HEAD_DIM), F32),
            pltpu.VMEM((nh, 1, 1), F32),
            pltpu.VMEM((CONV_HALO, 2 * d), F32),
            pltpu.VMEM((CONV_HALO + L, 2 * d), F32),
            pltpu.VMEM((L, d), BF16),
            pltpu.VMEM((L, 4 * d), F32),
            pltpu.VMEM((L, 4 * d), F32),
        ],
        compiler_params=_params("arbitrary", "arbitrary"),
        name="mlstm_layer",
    )(x, x, w_in, wg.astype(BF16), wgt.astype(BF16), bg_row, bg_col, conv_w.astype(F32),
      norm_g[None, :].astype(F32), w_out, g, b)


def kernel(x, rel_bias, ln_g, ln_b, ffn_w_gu, ffn_w_down, a_w_in, a_w_out, b_w_in, b_w_group, b_scale, b_w_out,
           c_w_in, c_b_gates, c_conv_w, c_norm_g, c_w_out):
    batch, seq, d = x.shape
    h = x.reshape(batch * seq, d)
    bf = lambda w: w.astype(BF16)
    for i in range(DEPTH):
        lg = lambda s: ln_g[i, s][None, :]
        lb = lambda s: ln_b[i, s][None, :]
        h = _ffn(h, ffn_w_gu, ffn_w_down, i, 0, lg(0), lb(0))
        kind, j = i % N_MIXERS, i // N_MIXERS
        pre = None
        if kind == 0:
            pre = (_moba_mixer(h, a_w_in[j], rel_bias, batch, seq), bf(a_w_out[j]), lg(1), lb(1))
        elif kind == 1:
            h = _pool_layer(h, bf(b_w_in[j]), bf(b_w_group[j]), b_scale[j][None, :], bf(b_w_out[j]),
                            lg(1), lb(1), seq)
        else:
            h = _mlstm_layer(h, bf(c_w_in[j]), c_b_gates[j], c_conv_w[j], c_norm_g[j], bf(c_w_out[j]),
                             lg(1), lb(1), batch, seq)
        h = _ffn(h, ffn_w_gu, ffn_w_down, i, 1, lg(2), lb(2), pre)
    return h.reshape(batch, seq, d)
```

```python
import functools
import math

import jax
import jax.numpy as jnp
from jax import lax
from jax.experimental import pallas as pl
from jax.experimental.pallas import tpu as pltpu

F32 = jnp.float32
BF16 = jnp.bfloat16

D_MODEL = 1024
DEPTH = 4
N_MIXERS = 3
D_FF = 2816
LN_EPS = 1e-5
ALPHA = (2 * DEPTH) ** 0.25
A_HEADS = 8
A_HEAD_DIM = D_MODEL // A_HEADS
MOBA_BLOCK = 256
MOBA_TOPK = 3
REL_BUCKETS = 32
REL_MAX_EXACT = REL_BUCKETS // 2
REL_MAX_DIST = 128
POOL_WINDOWS = (2, 4, 8, 16)
POOL_GROUP = D_MODEL // len(POOL_WINDOWS)
POOL_HALO = 16
C_HEADS = 4
C_HEAD_DIM = D_MODEL // C_HEADS
C_CONV = 4
C_CHUNK = 256
CONV_HALO = 8
GATE_PAD = 128

NEG = -1e30
LOG2E = math.log2(math.e)
MOBA_GROUP = 4
MOBA_HEADS_PER_STEP = 2
MOBA_EXP_ROWS = 64
MOBA_VT_ROWS = 128 + 16
V7X_VMEM_LIMIT = 56 * 1024 * 1024
FFN_CHUNK = 256
FFN_ROW_TILE = 1024
FFN_SUBTILE_ROWS = 256
FFN_STAGE_COLS = 512
FFN_STAGE_ROWS = 256
ROW_TILE = 512

NT_DIMS = (((1,), (1,)), ((), ()))


def _params(*sem):
    return pltpu.CompilerParams(dimension_semantics=sem, vmem_limit_bytes=V7X_VMEM_LIMIT)


def _const_spec(shape):
    nd = len(shape)
    return pl.BlockSpec(shape, lambda *_: (0,) * nd, pipeline_mode=pl.Buffered(1))


def _layer_norm(z, g, b):
    mu = jnp.mean(z, axis=-1, keepdims=True)
    zc = z - mu
    var = jnp.mean(zc * zc, axis=-1, keepdims=True)
    return zc * lax.rsqrt(var + LN_EPS) * g + b


def _sigmoid(x):
    return 1.0 / (1.0 + jnp.exp(-x))


def _dot(a, b):
    return jnp.dot(a, b, preferred_element_type=F32)


def _dot_nt(a, b):
    return lax.dot_general(a, b, NT_DIMS, preferred_element_type=F32)


def _stage_as_bf16(chunks, stage_ref, sem_ref):
    def copy(c):
        return pltpu.make_async_copy(chunks[c][0], stage_ref.at[c % 2], sem_ref.at[c % 2])

    for c in range(min(2, len(chunks))):
        copy(c).start()
    for c in range(len(chunks)):
        copy(c).wait()
        dst = chunks[c][1]
        dst[...] = stage_ref[c % 2].astype(BF16)
        if c + 2 < len(chunks):
            copy(c + 2).start()


def _ffn_kernel(*refs, has_pre, layer, slot):
    x_ref = refs[0]
    (wgu_hbm, wd_hbm, g_ref, b_ref, o_ref,
     xb_ref, h_ref, wgu_ref, wd_ref, stage_gu_ref, stage_d_ref, sem_gu_ref, sem_d_ref) = refs[1 + 4 * has_pre:]

    @pl.when(pl.program_id(0) == 0)
    def _():
        wc, wr = FFN_STAGE_COLS, FFN_STAGE_ROWS
        gu = wgu_hbm.at[layer, slot]
        dn = wd_hbm.at[layer, slot]
        _stage_as_bf16([(gu.at[:, pl.ds(c * wc, wc)], wgu_ref.at[:, pl.ds(c * wc, wc)])
                        for c in range(2 * D_FF // wc)], stage_gu_ref, sem_gu_ref)
        _stage_as_bf16([(dn.at[pl.ds(r * wr, wr), :], wd_ref.at[pl.ds(r * wr, wr), :])
                        for r in range(D_FF // wr)], stage_d_ref, sem_d_ref)

    sm = FFN_SUBTILE_ROWS
    nsub = x_ref.shape[0] // sm
    if has_pre:
        a_ref, wa_ref, ga_ref, ba_ref = refs[1:5]
        for s in range(nsub):
            rows = slice(s * sm, (s + 1) * sm)
            o_ref[rows, :] = _layer_norm(ALPHA * x_ref[rows, :] + _dot(a_ref[rows, :], wa_ref[...]),
                                         ga_ref[...], ba_ref[...])
    for s in range(nsub):
        rows = slice(s * sm, (s + 1) * sm)
        buf = s % 2
        x = o_ref[rows, :] if has_pre else x_ref[rows, :]
        xb_ref[buf] = x.astype(BF16)
        for c in range(D_FF // FFN_CHUNK):
            lo = c * FFN_CHUNK
            xb = xb_ref[buf]
            gate = _dot(xb, wgu_ref[:, lo:lo + FFN_CHUNK])
            up = _dot(xb, wgu_ref[:, D_FF + lo:D_FF + lo + FFN_CHUNK])
            h_ref[buf, :, lo:lo + FFN_CHUNK] = (gate * _sigmoid(gate) * up).astype(BF16)
        y = _dot(h_ref[buf], wd_ref[...])
        o_ref[rows, :] = _layer_norm(ALPHA * x + 0.5 * y, g_ref[...], b_ref[...])


def _ffn(x, w_gu_all, w_down_all, layer, slot, g, b, pre=None):
    t = x.shape[0]
    tm = FFN_ROW_TILE
    sm = FFN_SUBTILE_ROWS
    assert (2 * D_FF) % FFN_STAGE_COLS == 0 and D_FF % FFN_STAGE_ROWS == 0
    row = lambda i: (i, 0)
    vec = _const_spec((1, D_MODEL))
    hbm = pl.BlockSpec(memory_space=pl.ANY)
    in_specs = [pl.BlockSpec((tm, D_MODEL), row)]
    args = [x]
    if pre is not None:
        in_specs += [pl.BlockSpec((tm, D_MODEL), row), _const_spec((D_MODEL, D_MODEL)), vec, vec]
        args += list(pre)
    in_specs += [hbm, hbm, vec, vec]
    args += [w_gu_all, w_down_all, g, b]
    return pl.pallas_call(
        functools.partial(_ffn_kernel, has_pre=pre is not None, layer=layer, slot=slot),
        grid=(t // tm,),
        in_specs=in_specs,
        out_specs=pl.BlockSpec((tm, D_MODEL), row),
        out_shape=jax.ShapeDtypeStruct((t, D_MODEL), F32),
        scratch_shapes=[
            pltpu.VMEM((2, sm, D_MODEL), BF16),
            pltpu.VMEM((2, sm, D_FF), BF16),
            pltpu.VMEM((D_MODEL, 2 * D_FF), BF16),
            pltpu.VMEM((D_FF, D_MODEL), BF16),
            pltpu.VMEM((2, D_MODEL, FFN_STAGE_COLS), F32),
            pltpu.VMEM((2, FFN_STAGE_ROWS, D_MODEL), F32),
            pltpu.SemaphoreType.DMA((2,)),
            pltpu.SemaphoreType.DMA((2,)),
        ],
        compiler_params=_params("arbitrary"),
        name="ffn_pre" if pre is not None else "ffn",
    )(*args)


def _t5_bucket(dist):
    n = jnp.maximum(dist, 0)
    is_small = n < REL_MAX_EXACT
    nf = jnp.maximum(n, 1).astype(F32)
    large = REL_MAX_EXACT + (jnp.log(nf / REL_MAX_EXACT) / math.log(REL_MAX_DIST / REL_MAX_EXACT)
                             * (REL_BUCKETS - REL_MAX_EXACT)).astype(jnp.int32)
    large = jnp.minimum(large, REL_BUCKETS - 1)
    return jnp.where(is_small, n, large)


def _moba_bias_tables(rel_bias):
    blk = MOBA_BLOCK
    nh = rel_bias.shape[1]
    far = rel_bias[REL_BUCKETS - 1][:, None]

    def by_distance(dist):
        onehot = _t5_bucket(dist)[:, None] == jnp.arange(REL_BUCKETS)
        picked = jnp.sum(jnp.where(onehot[None], rel_bias.T[:, None, :], 0.0), axis=-1)
        return (picked - far) * LOG2E

    def toeplitz(v):
        flat = jnp.broadcast_to(v[:, None, :], (nh, blk, 2 * blk)).reshape(nh, 2 * blk * blk)
        skew = flat[:, blk - 1:blk - 1 + blk * (2 * blk - 1)].reshape(nh, blk, 2 * blk - 1)
        return skew[:, :, :blk]

    d = jnp.arange(2 * blk) - (blk - 1)
    own = toeplitz(jnp.where(d[None] >= 0, by_distance(d), NEG))
    adj = toeplitz(by_distance(d + blk))
    return jnp.stack([own, adj, jnp.zeros_like(adj)], axis=1).astype(F32)


def _moba_qkv_kernel(x_ref, wqk_ref, wvt_ref, qk_ref, vt_ref):
    xb = x_ref[...].astype(BF16)
    qk_ref[...] = _dot(xb, wqk_ref[...]).astype(BF16)
    vt = _dot_nt(wvt_ref[...], xb).astype(BF16)
    dh, rows = A_HEAD_DIM, MOBA_VT_ROWS
    for h in range(A_HEADS):
        vt_ref[h * rows:h * rows + dh, :] = vt[h * dh:(h + 1) * dh, :]
        vt_ref[h * rows + dh:(h + 1) * rows, :] = jnp.ones((rows - dh, vt.shape[1]), BF16)


def _moba_qkv(x, w_qk, w_vt, batch, seq):
    t = x.shape[0]
    tm = ROW_TILE
    tps = seq // tm
    return pl.pallas_call(
        _moba_qkv_kernel,
        grid=(t // tm,),
        in_specs=[
            pl.BlockSpec((tm, D_MODEL), lambda i: (i, 0)),
            _const_spec((D_MODEL, 2 * D_MODEL)),
            _const_spec((D_MODEL, D_MODEL)),
        ],
        out_specs=[
            pl.BlockSpec((tm, 2 * D_MODEL), lambda i: (i, 0)),
            pl.BlockSpec((None, A_HEADS * MOBA_VT_ROWS, tm), lambda i: (i // tps, 0, i % tps)),
        ],
        out_shape=[
            jax.ShapeDtypeStruct((t, 2 * D_MODEL), BF16),
            jax.ShapeDtypeStruct((batch, A_HEADS * MOBA_VT_ROWS, seq), BF16),
        ],
        compiler_params=_params("parallel"),
        name="moba_qkv",
    )(x, w_qk, w_vt)


def _moba_kernel(q_ref, qn_ref, k_ref, vt_ref, tab_ref, o_ref, kmean_ref, sel_ref, t_ref, p_ref, m_ref, *, nb):
    blk = MOBA_BLOCK
    dh = A_HEAD_DIM
    gb = MOBA_GROUP
    rows = MOBA_VT_ROWS
    sub = MOBA_EXP_ROWS
    heads = range(MOBA_HEADS_PER_STEP)
    s = pl.program_id(2)

    @pl.when(s == 0)
    def _():
        for hh in heads:
            for j in range(nb):
                kb = k_ref[j * blk:(j + 1) * blk, hh * dh:(hh + 1) * dh].astype(F32)
                kmean_ref[hh, j:j + 1, :] = jnp.mean(kb, axis=0, keepdims=True)
        o_ref[...] = jnp.zeros_like(o_ref)
        own = lax.broadcasted_iota(jnp.int32, (nb, blk), 0) == 0
        for hh in heads:
            sel_ref[hh] = jnp.where(own, 1.0, 0.0)

    def select_next():
        i = jnp.minimum(s + 1, nb - 1)
        jidx = lax.broadcasted_iota(jnp.int32, (nb, blk), 0)
        for hh in heads:
            q = qn_ref[:, hh * dh:(hh + 1) * dh]
            gate = _dot_nt(kmean_ref[hh].astype(BF16), q)
            cnt = jnp.zeros((nb, blk), F32)
            for jp in range(nb - 1):
                row = gate[jp:jp + 1, :]
                beats = (row > gate) | ((row == gate) & (jp < jidx))
                cnt = cnt + jnp.where(beats & (jp < i), 1.0, 0.0)
            chosen = ((cnt < MOBA_TOPK) & (jidx < i)) | (jidx == i)
            sel_ref[hh] = jnp.where(chosen, 1.0, 0.0)

    def step(ng_score, ng_finish):
        first_dynamic = (ng_score - 1) * gb - 1
        nk_finish = ng_finish * gb * blk
        if ng_finish:
            m_prev = [m_ref[hh] for hh in heads]
        m8 = [None] * len(heads)
        for j in range(max(ng_score, ng_finish) * gb):
            if j < ng_finish * gb:
                for hh in heads:
                    for r in range(j * blk, (j + 1) * blk, sub):
                        p_ref[hh, r:r + sub, :] = jnp.exp2(t_ref[hh, r:r + sub, :] - m_prev[hh]).astype(BF16)
            if j < ng_score * gb:
                for hh in heads:
                    t = _dot_nt(k_ref[j * blk:(j + 1) * blk, hh * dh:(hh + 1) * dh],
                                q_ref[:, hh * dh:(hh + 1) * dh])
                    if j >= first_dynamic:
                        t = t + tab_ref[hh, jnp.clip(s - j, 0, 2)]
                    t = jnp.where(sel_ref[hh, j:j + 1, :] > 0.5, t, NEG)
                    t_ref[hh, j * blk:(j + 1) * blk, :] = t
                    mb = jnp.max(t.reshape(blk // 8, 8, blk), axis=0)
                    m8[hh] = mb if j == 0 else jnp.maximum(m8[hh], mb)
        if ng_score:
            for hh in heads:
                m_ref[hh] = jnp.max(m8[hh], axis=0, keepdims=True)
        select_next()
        if ng_finish:
            for hh in heads:
                o_aug = _dot(vt_ref[hh * rows:(hh + 1) * rows, 0:nk_finish], p_ref[hh, 0:nk_finish, :])
                o_t = o_aug[0:dh, :] / o_aug[dh:dh + 1, :]
                o_ref[:, hh * dh:(hh + 1) * dh] = jnp.transpose(o_t).astype(o_ref.dtype)

    ng_score = jnp.where(s < nb, s // gb + 1, 0)
    ng_finish = jnp.where(s >= 1, (s - 1) // gb + 1, 0)
    combos = sorted({(q // gb + 1 if q < nb else 0, (q - 1) // gb + 1 if q >= 1 else 0) for q in range(nb + 1)})
    for a, b in combos:
        pl.when((ng_score == a) & (ng_finish == b))(functools.partial(step, a, b))


def _moba_attention(qk, vt, tab, batch, seq):
    blk = MOBA_BLOCK
    nb = seq // blk
    hp = MOBA_HEADS_PER_STEP
    w = hp * A_HEAD_DIM
    ngrp = A_HEADS // hp
    assert seq % blk == 0 and nb % MOBA_GROUP == 0 and A_HEADS % hp == 0
    kern = functools.partial(_moba_kernel, nb=nb)
    return pl.pallas_call(
        kern,
        grid=(batch, ngrp, nb + 1),
        in_specs=[
            pl.BlockSpec((blk, w), lambda b, h, s: (b * nb + jnp.minimum(s, nb - 1), h)),
            pl.BlockSpec((blk, w), lambda b, h, s: (b * nb + jnp.minimum(s + 1, nb - 1), h)),
            pl.BlockSpec((seq, w), lambda b, h, s: (b, ngrp + h)),
            pl.BlockSpec((None, hp * MOBA_VT_ROWS, seq), lambda b, h, s: (b, h, 0)),
            pl.BlockSpec((hp, 3, blk, blk), lambda b, h, s: (h, 0, 0, 0)),
        ],
        out_specs=pl.BlockSpec((blk, w), lambda b, h, s: (b * nb + jnp.maximum(s - 1, 0), h)),
        out_shape=jax.ShapeDtypeStruct((batch * seq, D_MODEL), BF16),
        scratch_shapes=[
            pltpu.VMEM((hp, nb, A_HEAD_DIM), F32),
            pltpu.VMEM((hp, nb, blk), F32),
            pltpu.VMEM((hp, seq, blk), F32),
            pltpu.VMEM((hp, seq, blk), BF16),
            pltpu.VMEM((hp, 1, blk), F32),
        ],
        compiler_params=_params("parallel", "parallel", "arbitrary"),
        name="moba_attn",
    )(qk, qk, qk, vt, tab)


def _moba_mixer(x, w_in, rel_bias, batch, seq):
    assert REL_MAX_DIST <= MOBA_BLOCK
    d = D_MODEL
    c1 = (A_HEAD_DIM ** -0.5) * LOG2E
    w_qk = jnp.concatenate([w_in[:, :d] * c1, w_in[:, d:2 * d]], axis=1).astype(BF16)
    w_vt = w_in[:, 2 * d:].T.astype(BF16)
    qk, vt = _moba_qkv(x, w_qk, w_vt, batch, seq)
    return _moba_attention(qk, vt, _moba_bias_tables(rel_bias), batch, seq)


def _pool_kernel(x_ref, halo_ref, win_ref, wgrp_ref, scale_ref, wout_ref, g_ref, b_ref, o_ref,
                 ubuf_ref, ybuf_ref, *, tiles_per_seq):
    tm = x_ref.shape[0]
    hl = POOL_HALO
    ti = pl.program_id(0) % tiles_per_seq
    x = x_ref[...]
    u_halo = _dot(halo_ref[...].astype(BF16), win_ref[...])
    ubuf_ref[0:hl, :] = jnp.where(ti == 0, 0.0, u_halo)
    ubuf_ref[hl:hl + tm, :] = _dot(x.astype(BF16), win_ref[...])
    pos = ti * tm + lax.broadcasted_iota(jnp.int32, (tm, POOL_GROUP), 0)
    for gi, w in enumerate(POOL_WINDOWS):
        lo = gi * POOL_GROUP
        u = ubuf_ref[hl:hl + tm, lo:lo + POOL_GROUP]
        ws = u
        for d in range(1, w):
            ws = ws + ubuf_ref[hl - d:hl - d + tm, lo:lo + POOL_GROUP]
        cnt = jnp.minimum(pos + 1, w).astype(F32)
        pooled = ws / cnt - u
        yg = _dot(pooled.astype(BF16), wgrp_ref[gi]) * scale_ref[:, lo:lo + POOL_GROUP]
        ybuf_ref[:, lo:lo + POOL_GROUP] = yg.astype(BF16)
    y = _dot(ybuf_ref[...], wout_ref[...])
    o_ref[...] = _layer_norm(ALPHA * x + y, g_ref[...], b_ref[...])


def _pool_layer(x, w_in, w_group, scale, w_out, g, b, seq):
    t = x.shape[0]
    tm = ROW_TILE
    hl = POOL_HALO
    assert seq % tm == 0 and tm % hl == 0 and max(POOL_WINDOWS) <= hl
    kern = functools.partial(_pool_kernel, tiles_per_seq=seq // tm)
    ng = len(POOL_WINDOWS)
    return pl.pallas_call(
        kern,
        grid=(t // tm,),
        in_specs=[
            pl.BlockSpec((tm, D_MODEL), lambda i: (i, 0)),
            pl.BlockSpec((hl, D_MODEL), lambda i: (jnp.maximum(i * (tm // hl) - 1, 0), 0)),
            _const_spec((D_MODEL, D_MODEL)),
            _const_spec((ng, POOL_GROUP, POOL_GROUP)),
            _const_spec((1, D_MODEL)),
            _const_spec((D_MODEL, D_MODEL)),
            _const_spec((1, D_MODEL)),
            _const_spec((1, D_MODEL)),
        ],
        out_specs=pl.BlockSpec((tm, D_MODEL), lambda i: (i, 0)),
        out_shape=jax.ShapeDtypeStruct((t, D_MODEL), F32),
        scratch_shapes=[pltpu.VMEM((tm + hl, D_MODEL), F32), pltpu.VMEM((tm, D_MODEL), BF16)],
        compiler_params=_params("parallel"),
        name="pool_layer",
    )(x, x, w_in, w_group, scale, w_out, g, b)


def _log_sigmoid(x):
    return jnp.minimum(x, 0.0) - jnp.log1p(jnp.exp(-jnp.abs(x)))


def _mlstm_kernel(xn_ref, xc_ref, win_ref, wg_ref, wgt_ref, bg_row_ref, bg_col_ref, cw_ref,
                  ng_ref, wout_ref, g_ref, b_ref, o_ref,
                  c_ref, n_ref, m_ref, cprev_ref, cbuf_ref, hn_ref, pa_ref, pb_ref):
    c = pl.program_id(1)

    @pl.when(c == 0)
    def _():
        pb_ref[...] = jnp.zeros_like(pb_ref)

    @pl.when(c <= 1)
    def _():
        c_ref[...] = jnp.zeros_like(c_ref)
        n_ref[...] = jnp.zeros_like(n_ref)
        m_ref[...] = jnp.zeros_like(m_ref)
        cprev_ref[...] = jnp.zeros_like(cprev_ref)

    pl.when(c % 2 == 0)(functools.partial(
        _mlstm_step, pa_ref, pb_ref, xn_ref, xc_ref, win_ref, wg_ref, wgt_ref, bg_row_ref, bg_col_ref, cw_ref,
        ng_ref, wout_ref, g_ref, b_ref, o_ref, c_ref, n_ref, m_ref, cprev_ref, cbuf_ref, hn_ref))
    pl.when(c % 2 == 1)(functools.partial(
        _mlstm_step, pb_ref, pa_ref, xn_ref, xc_ref, win_ref, wg_ref, wgt_ref, bg_row_ref, bg_col_ref, cw_ref,
        ng_ref, wout_ref, g_ref, b_ref, o_ref, c_ref, n_ref, m_ref, cprev_ref, cbuf_ref, hn_ref))


def _mlstm_step(p_write, p_read, xn_ref, xc_ref, win_ref, wg_ref, wgt_ref, bg_row_ref, bg_col_ref, cw_ref,
                ng_ref, wout_ref, g_ref, b_ref, o_ref, c_ref, n_ref, m_ref, cprev_ref, cbuf_ref, hn_ref):
    L = C_CHUNK
    d = D_MODEL
    dh = C_HEAD_DIM
    nh = C_HEADS
    hl = CONV_HALO
    pw = 4 * d // nh

    xnb = xn_ref[...].astype(BF16)
    x = xc_ref[...]
    xb = x.astype(BF16)
    g_col = _dot(xb, wg_ref[...]) + bg_row_ref[...]
    g_row = _dot_nt(wgt_ref[...], xb) + bg_col_ref[...]

    cbuf_ref[0:hl, :] = cprev_ref[...]
    cbuf_ref[hl:hl + L, :] = p_read[:, 0:2 * d]
    cprev_ref[...] = p_read[L - hl:L, 0:2 * d]
    conv = cw_ref[C_CONV - 1:C_CONV, :] * p_read[:, 0:2 * d]
    for j in range(C_CONV - 2, -1, -1):
        off = hl - (C_CONV - 1) + j
        conv = conv + cw_ref[j:j + 1, :] * cbuf_ref[off:off + L, :]
    qk = conv * _sigmoid(conv)

    r = lax.broadcasted_iota(jnp.int32, (L, L), 0)
    cc = lax.broadcasted_iota(jnp.int32, (L, L), 1)
    lower = cc <= r

    for hd in range(nh):
        p_write[:, hd * pw:(hd + 1) * pw] = _dot(xnb, win_ref[:, hd * pw:(hd + 1) * pw])

        i_col = g_col[:, hd:hd + 1]
        i_row = g_row[hd:hd + 1, :]
        lf_col = _log_sigmoid(g_col[:, nh + hd:nh + hd + 1])
        lf_row = _log_sigmoid(g_row[nh + hd:nh + hd + 1, :])
        b_col = jnp.sum(jnp.where(lower, lf_row, 0.0), axis=1, keepdims=True)
        b_row = jnp.sum(jnp.where(r <= cc, lf_col, 0.0), axis=0, keepdims=True)
        b_last = jnp.sum(lf_row, axis=1, keepdims=True)

        q_h = qk[:, hd * dh:(hd + 1) * dh]
        k_h = qk[:, d + hd * dh:d + (hd + 1) * dh] * (dh ** -0.5)
        qb = q_h.astype(BF16)
        kb = k_h.astype(BF16)
        vb = p_read[:, 2 * d + hd * dh:2 * d + (hd + 1) * dh].astype(BF16)
        c_st = c_ref[hd]
        n_st = n_ref[hd]
        m_prev = m_ref[hd]

        d_intra = jnp.where(lower, b_col - b_row + i_row, NEG)
        m_inter = b_col + m_prev
        m_t = jnp.maximum(m_inter, jnp.max(d_intra, axis=1, keepdims=True))
        w = jnp.exp(d_intra - m_t) * _dot_nt(qb, kb)
        s_inter = jnp.exp(m_inter - m_t)
        num = s_inter * _dot(qb, c_st.astype(BF16)) + _dot(w.astype(BF16), vb)
        den = s_inter * jnp.sum(q_h * n_st, axis=1, keepdims=True) + jnp.sum(w, axis=1, keepdims=True)
        ht = num / jnp.maximum(jnp.abs(den), jnp.exp(-m_t))

        gg_col = b_last - b_col + i_col
        gg_row = b_last - b_row + i_row
        m_new = jnp.maximum(b_last + m_prev, jnp.max(gg_row, axis=1, keepdims=True))
        decay = jnp.exp(b_last + m_prev - m_new)
        kw = k_h * jnp.exp(gg_col - m_new)
        c_ref[hd] = decay * c_st + _dot(jnp.transpose(kw).astype(BF16), vb)
        n_ref[hd] = decay * n_st + jnp.sum(kw, axis=0, keepdims=True)
        m_ref[hd] = m_new

        hc = _sigmoid(p_read[:, 3 * d + hd * dh:3 * d + (hd + 1) * dh]) * ht
        mu = jnp.mean(hc, axis=1, keepdims=True)
        hcc = hc - mu
        var = jnp.mean(hcc * hcc, axis=1, keepdims=True)
        hn = hcc * lax.rsqrt(var + LN_EPS) * ng_ref[:, hd * dh:(hd + 1) * dh]
        hn_ref[:, hd * dh:(hd + 1) * dh] = hn.astype(BF16)

    y = _dot(hn_ref[...], wout_ref[...])
    o_ref[...] = _layer_norm(ALPHA * x + y, g_ref[...], b_ref[...])


def _mlstm_layer(x, w_in, b_gates, conv_w, norm_g, w_out, g, b, batch, seq):
    t = x.shape[0]
    L = C_CHUNK
    nc = seq // L
    d = D_MODEL
    nh = C_HEADS
    assert seq % L == 0
    w_gate = w_in[:, 4 * d:]
    wg = jnp.pad(w_gate, ((0, 0), (0, GATE_PAD - 2 * nh)))
    wgt = jnp.pad(w_gate.T, ((0, 16 - 2 * nh), (0, 0)))
    bg_row = jnp.pad(b_gates, (0, GATE_PAD - 2 * nh))[None, :].astype(F32)
    bg_col = jnp.pad(b_gates, (0, 16 - 2 * nh))[:, None].astype(F32)
    return pl.pallas_call(
        _mlstm_kernel,
        grid=(batch, nc + 1),
        in_specs=[
            pl.BlockSpec((L, d), lambda bb, c: (bb * nc + jnp.minimum(c, nc - 1), 0)),
            pl.BlockSpec((L, d), lambda bb, c: (bb * nc + jnp.maximum(c - 1, 0), 0)),
            pl.BlockSpec((d, 4 * d), lambda bb, c: (0, 0), pipeline_mode=pl.Buffered(1)),
            _const_spec((d, GATE_PAD)),
            _const_spec((16, d)),
            _const_spec((1, GATE_PAD)),
            _const_spec((16, 1)),
            _const_spec((C_CONV, 2 * d)),
            _const_spec((1, d)),
            _const_spec((d, d)),
            _const_spec((1, d)),
            _const_spec((1, d)),
        ],
        out_specs=pl.BlockSpec((L, d), lambda bb, c: (bb * nc + jnp.maximum(c - 1, 0), 0)),
        out_shape=jax.ShapeDtypeStruct((t, d), F32),
        scratch_shapes=[
            pltpu.VMEM((nh, C_HEAD_DIM, C_HEAD_DIM), F32),
            pltpu.VMEM((nh, 1, C_HEAD_DIM), F32),
            pltpu.VMEM((nh, 1, 1), F32),
            pltpu.VMEM((CONV_HALO, 2 * d), F32),
            pltpu.VMEM((CONV_HALO + L, 2 * d), F32),
            pltpu.VMEM((L, d), BF16),
            pltpu.VMEM((L, 4 * d), F32),
            pltpu.VMEM((L, 4 * d), F32),
        ],
        compiler_params=_params("arbitrary", "arbitrary"),
        name="mlstm_layer",
    )(x, x, w_in, wg.astype(BF16), wgt.astype(BF16), bg_row, bg_col, conv_w.astype(F32),
      norm_g[None, :].astype(F32), w_out, g, b)


def kernel(x, rel_bias, ln_g, ln_b, ffn_w_gu, ffn_w_down, a_w_in, a_w_out, b_w_in, b_w_group, b_scale, b_w_out,
           c_w_in, c_b_gates, c_conv_w, c_norm_g, c_w_out):
    batch, seq, d = x.shape
    h = x.reshape(batch * seq, d)
    bf = lambda w: w.astype(BF16)
    for i in range(DEPTH):
        lg = lambda s: ln_g[i, s][None, :]
        lb = lambda s: ln_b[i, s][None, :]
        h = _ffn(h, ffn_w_gu, ffn_w_down, i, 0, lg(0), lb(0))
        kind, j = i % N_MIXERS, i // N_MIXERS
        pre = None
        if kind == 0:
            pre = (_moba_mixer(h, a_w_in[j], rel_bias, batch, seq), bf(a_w_out[j]), lg(1), lb(1))
        elif kind == 1:
            h = _pool_layer(h, bf(b_w_in[j]), bf(b_w_group[j]), b_scale[j][None, :], bf(b_w_out[j]),
                            lg(1), lb(1), seq)
        else:
            h = _mlstm_layer(h, bf(c_w_in[j]), c_b_gates[j], c_conv_w[j], c_norm_g[j], bf(c_w_out[j]),
                             lg(1), lb(1), batch, seq)
        h = _ffn(h, ffn_w_gu, ffn_w_down, i, 1, lg(2), lb(2), pre)
    return h.reshape(batch, seq, d)
```

```python
import functools
import math

import jax
import jax.numpy as jnp
from jax import lax
from jax.experimental import pallas as pl
from jax.experimental.pallas import tpu as pltpu

F32 = jnp.float32
BF16 = jnp.bfloat16

D_MODEL = 1024
DEPTH = 4
N_MIXERS = 3
D_FF = 2816
LN_EPS = 1e-5
ALPHA = (2 * DEPTH) ** 0.25
A_HEADS = 8
A_HEAD_DIM = D_MODEL // A_HEADS
MOBA_BLOCK = 256
MOBA_TOPK = 3
REL_BUCKETS = 32
REL_MAX_EXACT = REL_BUCKETS // 2
REL_MAX_DIST = 128
POOL_WINDOWS = (2, 4, 8, 16)
POOL_GROUP = D_MODEL // len(POOL_WINDOWS)
POOL_HALO = 16
C_HEADS = 4
C_HEAD_DIM = D_MODEL // C_HEADS
C_CONV = 4
C_CHUNK = 256
CONV_HALO = 8
GATE_PAD = 128

NEG = -1e30
LOG2E = math.log2(math.e)
MOBA_GROUP = 4
MOBA_HEADS_PER_STEP = 2
MOBA_EXP_ROWS = 64
MOBA_VT_ROWS = 128 + 16
V7X_VMEM_LIMIT = 56 * 1024 * 1024
FFN_CHUNK = 256
FFN_ROW_TILE = 1024
FFN_SUBTILE_ROWS = 256
FFN_STAGE_COLS = 512
FFN_STAGE_ROWS = 256
ROW_TILE = 512

NT_DIMS = (((1,), (1,)), ((), ()))


def _params(*sem):
    return pltpu.CompilerParams(dimension_semantics=sem, vmem_limit_bytes=V7X_VMEM_LIMIT)


def _const_spec(shape):
    nd = len(shape)
    return pl.BlockSpec(shape, lambda *_: (0,) * nd, pipeline_mode=pl.Buffered(1))


def _layer_norm(z, g, b):
    mu = jnp.mean(z, axis=-1, keepdims=True)
    zc = z - mu
    var = jnp.mean(zc * zc, axis=-1, keepdims=True)
    return zc * lax.rsqrt(var + LN_EPS) * g + b


def _sigmoid(x):
    return 1.0 / (1.0 + jnp.exp(-x))


def _dot(a, b):
    return jnp.dot(a, b, preferred_element_type=F32)


def _dot_nt(a, b):
    return lax.dot_general(a, b, NT_DIMS, preferred_element_type=F32)


def _stage_as_bf16(chunks, stage_ref, sem_ref):
    def copy(c):
        return pltpu.make_async_copy(chunks[c][0], stage_ref.at[c % 2], sem_ref.at[c % 2])

    for c in range(min(2, len(chunks))):
        copy(c).start()
    for c in range(len(chunks)):
        copy(c).wait()
        dst = chunks[c][1]
        dst[...] = stage_ref[c % 2].astype(BF16)
        if c + 2 < len(chunks):
            copy(c + 2).start()


def _ffn_kernel(*refs, has_pre, layer, slot):
    x_ref = refs[0]
    (wgu_hbm, wd_hbm, g_ref, b_ref, o_ref,
     xb_ref, h_ref, wgu_ref, wd_ref, stage_gu_ref, stage_d_ref, sem_gu_ref, sem_d_ref) = refs[1 + 4 * has_pre:]

    @pl.when(pl.program_id(0) == 0)
    def _():
        wc, wr = FFN_STAGE_COLS, FFN_STAGE_ROWS
        gu = wgu_hbm.at[layer, slot]
        dn = wd_hbm.at[layer, slot]
        _stage_as_bf16([(gu.at[:, pl.ds(c * wc, wc)], wgu_ref.at[:, pl.ds(c * wc, wc)])
                        for c in range(2 * D_FF // wc)], stage_gu_ref, sem_gu_ref)
        _stage_as_bf16([(dn.at[pl.ds(r * wr, wr), :], wd_ref.at[pl.ds(r * wr, wr), :])
                        for r in range(D_FF // wr)], stage_d_ref, sem_d_ref)

    sm = FFN_SUBTILE_ROWS
    nsub = x_ref.shape[0] // sm
    if has_pre:
        a_ref, wa_ref, ga_ref, ba_ref = refs[1:5]
        for s in range(nsub):
            rows = slice(s * sm, (s + 1) * sm)
            o_ref[rows, :] = _layer_norm(ALPHA * x_ref[rows, :] + _dot(a_ref[rows, :], wa_ref[...]),
                                         ga_ref[...], ba_ref[...])
    for s in range(nsub):
        rows = slice(s * sm, (s + 1) * sm)
        buf = s % 2
        x = o_ref[rows, :] if has_pre else x_ref[rows, :]
        xb_ref[buf] = x.astype(BF16)
        for c in range(D_FF // FFN_CHUNK):
            lo = c * FFN_CHUNK
            xb = xb_ref[buf]
            gate = _dot(xb, wgu_ref[:, lo:lo + FFN_CHUNK])
            up = _dot(xb, wgu_ref[:, D_FF + lo:D_FF + lo + FFN_CHUNK])
            h_ref[buf, :, lo:lo + FFN_CHUNK] = (gate * _sigmoid(gate) * up).astype(BF16)
        y = _dot(h_ref[buf], wd_ref[...])
        o_ref[rows, :] = _layer_norm(ALPHA * x + 0.5 * y, g_ref[...], b_ref[...])


def _ffn(x, w_gu_all, w_down_all, layer, slot, g, b, pre=None):
    t = x.shape[0]
    tm = FFN_ROW_TILE
    sm = FFN_SUBTILE_ROWS
    assert (2 * D_FF) % FFN_STAGE_COLS == 0 and D_FF % FFN_STAGE_ROWS == 0
    row = lambda i: (i, 0)
    vec = _const_spec((1, D_MODEL))
    hbm = pl.BlockSpec(memory_space=pl.ANY)
    in_specs = [pl.BlockSpec((tm, D_MODEL), row)]
    args = [x]
    if pre is not None:
        in_specs += [pl.BlockSpec((tm, D_MODEL), row), _const_spec((D_MODEL, D_MODEL)), vec, vec]
        args += list(pre)
    in_specs += [hbm, hbm, vec, vec]
    args += [w_gu_all, w_down_all, g, b]
    return pl.pallas_call(
        functools.partial(_ffn_kernel, has_pre=pre is not None, layer=layer, slot=slot),
        grid=(t // tm,),
        in_specs=in_specs,
        out_specs=pl.BlockSpec((tm, D_MODEL), row),
        out_shape=jax.ShapeDtypeStruct((t, D_MODEL), F32),
        scratch_shapes=[
            pltpu.VMEM((2, sm, D_MODEL), BF16),
            pltpu.VMEM((2, sm, D_FF), BF16),
            pltpu.VMEM((D_MODEL, 2 * D_FF), BF16),
            pltpu.VMEM((D_FF, D_MODEL), BF16),
            pltpu.VMEM((2, D_MODEL, FFN_STAGE_COLS), F32),
            pltpu.VMEM((2, FFN_STAGE_ROWS, D_MODEL), F32),
            pltpu.SemaphoreType.DMA((2,)),
            pltpu.SemaphoreType.DMA((2,)),
        ],
        compiler_params=_params("arbitrary"),
        name="ffn_pre" if pre is not None else "ffn",
    )(*args)


def _t5_bucket(dist):
    n = jnp.maximum(dist, 0)
    is_small = n < REL_MAX_EXACT
    nf = jnp.maximum(n, 1).astype(F32)
    large = REL_MAX_EXACT + (jnp.log(nf / REL_MAX_EXACT) / math.log(REL_MAX_DIST / REL_MAX_EXACT)
                             * (REL_BUCKETS - REL_MAX_EXACT)).astype(jnp.int32)
    large = jnp.minimum(large, REL_BUCKETS - 1)
    return jnp.where(is_small, n, large)


def _moba_bias_tables(rel_bias):
    blk = MOBA_BLOCK
    nh = rel_bias.shape[1]
    far = rel_bias[REL_BUCKETS - 1][:, None]

    def by_distance(dist):
        onehot = _t5_bucket(dist)[:, None] == jnp.arange(REL_BUCKETS)
        picked = jnp.sum(jnp.where(onehot[None], rel_bias.T[:, None, :], 0.0), axis=-1)
        return (picked - far) * LOG2E

    def toeplitz(v):
        flat = jnp.broadcast_to(v[:, None, :], (nh, blk, 2 * blk)).reshape(nh, 2 * blk * blk)
        skew = flat[:, blk - 1:blk - 1 + blk * (2 * blk - 1)].reshape(nh, blk, 2 * blk - 1)
        return skew[:, :, :blk]

    d = jnp.arange(2 * blk) - (blk - 1)
    own = toeplitz(jnp.where(d[None] >= 0, by_distance(d), NEG))
    adj = toeplitz(by_distance(d + blk))
    return jnp.stack([own, adj, jnp.zeros_like(adj)], axis=1).astype(F32)


def _moba_qkv_kernel(x_ref, wqk_ref, wvt_ref, qk_ref, vt_ref):
    xb = x_ref[...].astype(BF16)
    qk_ref[...] = _dot(xb, wqk_ref[...]).astype(BF16)
    vt = _dot_nt(wvt_ref[...], xb).astype(BF16)
    dh, rows = A_HEAD_DIM, MOBA_VT_ROWS
    for h in range(A_HEADS):
        vt_ref[h * rows:h * rows + dh, :] = vt[h * dh:(h + 1) * dh, :]
        vt_ref[h * rows + dh:(h + 1) * rows, :] = jnp.ones((rows - dh, vt.shape[1]), BF16)


def _moba_qkv(x, w_qk, w_vt, batch, seq):
    t = x.shape[0]
    tm = ROW_TILE
    tps = seq // tm
    return pl.pallas_call(
        _moba_qkv_kernel,
        grid=(t // tm,),
        in_specs=[
            pl.BlockSpec((tm, D_MODEL), lambda i: (i, 0)),
            _const_spec((D_MODEL, 2 * D_MODEL)),
            _const_spec((D_MODEL, D_MODEL)),
        ],
        out_specs=[
            pl.BlockSpec((tm, 2 * D_MODEL), lambda i: (i, 0)),
            pl.BlockSpec((None, A_HEADS * MOBA_VT_ROWS, tm), lambda i: (i // tps, 0, i % tps)),
        ],
        out_shape=[
            jax.ShapeDtypeStruct((t, 2 * D_MODEL), BF16),
            jax.ShapeDtypeStruct((batch, A_HEADS * MOBA_VT_ROWS, seq), BF16),
        ],
        compiler_params=_params("parallel"),
        name="moba_qkv",
    )(x, w_qk, w_vt)


def _moba_kernel(q_ref, qn_ref, k_ref, vt_ref, tab_ref, o_ref, kmean_ref, sel_ref, t_ref, p_ref, m_ref, *, nb):
    blk = MOBA_BLOCK
    dh = A_HEAD_DIM
    gb = MOBA_GROUP
    rows = MOBA_VT_ROWS
    sub = MOBA_EXP_ROWS
    heads = range(MOBA_HEADS_PER_STEP)
    s = pl.program_id(2)

    @pl.when(s == 0)
    def _():
        for hh in heads:
            for j in range(nb):
                kb = k_ref[j * blk:(j + 1) * blk, hh * dh:(hh + 1) * dh].astype(F32)
                kmean_ref[hh, j:j + 1, :] = jnp.mean(kb, axis=0, keepdims=True)
        o_ref[...] = jnp.zeros_like(o_ref)
        own = lax.broadcasted_iota(jnp.int32, (nb, blk), 0) == 0
        for hh in heads:
            sel_ref[hh] = jnp.where(own, 1.0, 0.0)

    def select_next():
        i = jnp.minimum(s + 1, nb - 1)
        jidx = lax.broadcasted_iota(jnp.int32, (nb, blk), 0)
        for hh in heads:
            q = qn_ref[:, hh * dh:(hh + 1) * dh]
            gate = _dot_nt(kmean_ref[hh].astype(BF16), q)
            cnt = jnp.zeros((nb, blk), F32)
            for jp in range(nb - 1):
                row = gate[jp:jp + 1, :]
                beats = (row > gate) | ((row == gate) & (jp < jidx))
                cnt = cnt + jnp.where(beats & (jp < i), 1.0, 0.0)
            chosen = ((cnt < MOBA_TOPK) & (jidx < i)) | (jidx == i)
            sel_ref[hh] = jnp.where(chosen, 1.0, 0.0)

    def step(ng_score, ng_finish):
        first_dynamic = (ng_score - 1) * gb - 1
        nk_finish = ng_finish * gb * blk
        if ng_finish:
            m_prev = [m_ref[hh] for hh in heads]
        m8 = [None] * len(heads)
        for j in range(max(ng_score, ng_finish) * gb):
            if j < ng_finish * gb:
                for hh in heads:
                    for r in range(j * blk, (j + 1) * blk, sub):
                        p_ref[hh, r:r + sub, :] = jnp.exp2(t_ref[hh, r:r + sub, :] - m_prev[hh]).astype(BF16)
            if j < ng_score * gb:
                for hh in heads:
                    t = _dot_nt(k_ref[j * blk:(j + 1) * blk, hh * dh:(hh + 1) * dh],
                                q_ref[:, hh * dh:(hh + 1) * dh])
                    if j >= first_dynamic:
                        t = t + tab_ref[hh, jnp.clip(s - j, 0, 2)]
                    t = jnp.where(sel_ref[hh, j:j + 1, :] > 0.5, t, NEG)
                    t_ref[hh, j * blk:(j + 1) * blk, :] = t
                    mb = jnp.max(t.reshape(blk // 8, 8, blk), axis=0)
                    m8[hh] = mb if j == 0 else jnp.maximum(m8[hh], mb)
        if ng_score:
            for hh in heads:
                m_ref[hh] = jnp.max(m8[hh], axis=0, keepdims=True)
        select_next()
        if ng_finish:
            for hh in heads:
                o_aug = _dot(vt_ref[hh * rows:(hh + 1) * rows, 0:nk_finish], p_ref[hh, 0:nk_finish, :])
                o_t = o_aug[0:dh, :] / o_aug[dh:dh + 1, :]
                o_ref[:, hh * dh:(hh + 1) * dh] = jnp.transpose(o_t).astype(o_ref.dtype)

    ng_score = jnp.where(s < nb, s // gb + 1, 0)
    ng_finish = jnp.where(s >= 1, (s - 1) // gb + 1, 0)
    combos = sorted({(q // gb + 1 if q < nb else 0, (q - 1) // gb + 1 if q >= 1 else 0) for q in range(nb + 1)})
    for a, b in combos:
        pl.when((ng_score == a) & (ng_finish == b))(functools.partial(step, a, b))


def _moba_attention(qk, vt, tab, batch, seq):
    blk = MOBA_BLOCK
    nb = seq // blk
    hp = MOBA_HEADS_PER_STEP
    w = hp * A_HEAD_DIM
    ngrp = A_HEADS // hp
    assert seq % blk == 0 and nb % MOBA_GROUP == 0 and A_HEADS % hp == 0
    kern = functools.partial(_moba_kernel, nb=nb)
    return pl.pallas_call(
        kern,
        grid=(batch, ngrp, nb + 1),
        in_specs=[
            pl.BlockSpec((blk, w), lambda b, h, s: (b * nb + jnp.minimum(s, nb - 1), h)),
            pl.BlockSpec((blk, w), lambda b, h, s: (b * nb + jnp.minimum(s + 1, nb - 1), h)),
            pl.BlockSpec((seq, w), lambda b, h, s: (b, ngrp + h)),
            pl.BlockSpec((None, hp * MOBA_VT_ROWS, seq), lambda b, h, s: (b, h, 0)),
            pl.BlockSpec((hp, 3, blk, blk), lambda b, h, s: (h, 0, 0, 0)),
        ],
        out_specs=pl.BlockSpec((blk, w), lambda b, h, s: (b * nb + jnp.maximum(s - 1, 0), h)),
        out_shape=jax.ShapeDtypeStruct((batch * seq, D_MODEL), BF16),
        scratch_shapes=[
            pltpu.VMEM((hp, nb, A_HEAD_DIM), F32),
            pltpu.VMEM((hp, nb, blk), F32),
            pltpu.VMEM((hp, seq, blk), F32),
            pltpu.VMEM((hp, seq, blk), BF16),
            pltpu.VMEM((hp, 1, blk), F32),
        ],
        compiler_params=_params("parallel", "parallel", "arbitrary"),
        name="moba_attn",
    )(qk, qk, qk, vt, tab)


def _moba_mixer(x, w_in, rel_bias, batch, seq):
    assert REL_MAX_DIST <= MOBA_BLOCK
    d = D_MODEL
    c1 = (A_HEAD_DIM ** -0.5) * LOG2E
    w_qk = jnp.concatenate([w_in[:, :d] * c1, w_in[:, d:2 * d]], axis=1).astype(BF16)
    w_vt = w_in[:, 2 * d:].T.astype(BF16)
    qk, vt = _moba_qkv(x, w_qk, w_vt, batch, seq)
    return _moba_attention(qk, vt, _moba_bias_tables(rel_bias), batch, seq)


def _pool_kernel(x_ref, halo_ref, win_ref, wgrp_ref, scale_ref, wout_ref, g_ref, b_ref, o_ref,
                 ubuf_ref, ybuf_ref, *, tiles_per_seq):
    tm = x_ref.shape[0]
    hl = POOL_HALO
    ti = pl.program_id(0) % tiles_per_seq
    x = x_ref[...]
    u_halo = _dot(halo_ref[...].astype(BF16), win_ref[...])
    ubuf_ref[0:hl, :] = jnp.where(ti == 0, 0.0, u_halo)
    ubuf_ref[hl:hl + tm, :] = _dot(x.astype(BF16), win_ref[...])
    pos = ti * tm + lax.broadcasted_iota(jnp.int32, (tm, POOL_GROUP), 0)
    for gi, w in enumerate(POOL_WINDOWS):
        lo = gi * POOL_GROUP
        u = ubuf_ref[hl:hl + tm, lo:lo + POOL_GROUP]
        acc = ubuf_ref[:, lo:lo + POOL_GROUP]
        d = 1
        while d < w:
            acc = acc + pltpu.roll(acc, d, 0)
            d *= 2
        ws = acc[hl:hl + tm, :]
        cnt = jnp.minimum(pos + 1, w).astype(F32)
        pooled = ws / cnt - u
        yg = _dot(pooled.astype(BF16), wgrp_ref[gi]) * scale_ref[:, lo:lo + POOL_GROUP]
        ybuf_ref[:, lo:lo + POOL_GROUP] = yg.astype(BF16)
    y = _dot(ybuf_ref[...], wout_ref[...])
    o_ref[...] = _layer_norm(ALPHA * x + y, g_ref[...], b_ref[...])


def _pool_layer(x, w_in, w_group, scale, w_out, g, b, seq):
    t = x.shape[0]
    tm = ROW_TILE
    hl = POOL_HALO
    assert seq % tm == 0 and tm % hl == 0 and max(POOL_WINDOWS) <= hl
    kern = functools.partial(_pool_kernel, tiles_per_seq=seq // tm)
    ng = len(POOL_WINDOWS)
    return pl.pallas_call(
        kern,
        grid=(t // tm,),
        in_specs=[
            pl.BlockSpec((tm, D_MODEL), lambda i: (i, 0)),
            pl.BlockSpec((hl, D_MODEL), lambda i: (jnp.maximum(i * (tm // hl) - 1, 0), 0)),
            _const_spec((D_MODEL, D_MODEL)),
            _const_spec((ng, POOL_GROUP, POOL_GROUP)),
            _const_spec((1, D_MODEL)),
            _const_spec((D_MODEL, D_MODEL)),
            _const_spec((1, D_MODEL)),
            _const_spec((1, D_MODEL)),
        ],
        out_specs=pl.BlockSpec((tm, D_MODEL), lambda i: (i, 0)),
        out_shape=jax.ShapeDtypeStruct((t, D_MODEL), F32),
        scratch_shapes=[pltpu.VMEM((tm + hl, D_MODEL), F32), pltpu.VMEM((tm, D_MODEL), BF16)],
        compiler_params=_params("parallel"),
        name="pool_layer",
    )(x, x, w_in, w_group, scale, w_out, g, b)


def _log_sigmoid(x):
    return jnp.minimum(x, 0.0) - jnp.log1p(jnp.exp(-jnp.abs(x)))


def _mlstm_kernel(xn_ref, xc_ref, win_ref, wg_ref, wgt_ref, bg_row_ref, bg_col_ref, cw_ref,
                  ng_ref, wout_ref, g_ref, b_ref, o_ref,
                  c_ref, n_ref, m_ref, cprev_ref, cbuf_ref, hn_ref, pa_ref, pb_ref):
    c = pl.program_id(1)

    @pl.when(c == 0)
    def _():
        pb_ref[...] = jnp.zeros_like(pb_ref)

    @pl.when(c <= 1)
    def _():
        c_ref[...] = jnp.zeros_like(c_ref)
        n_ref[...] = jnp.zeros_like(n_ref)
        m_ref[...] = jnp.zeros_like(m_ref)
        cprev_ref[...] = jnp.zeros_like(cprev_ref)

    pl.when(c % 2 == 0)(functools.partial(
        _mlstm_step, pa_ref, pb_ref, xn_ref, xc_ref, win_ref, wg_ref, wgt_ref, bg_row_ref, bg_col_ref, cw_ref,
        ng_ref, wout_ref, g_ref, b_ref, o_ref, c_ref, n_ref, m_ref, cprev_ref, cbuf_ref, hn_ref))
    pl.when(c % 2 == 1)(functools.partial(
        _mlstm_step, pb_ref, pa_ref, xn_ref, xc_ref, win_ref, wg_ref, wgt_ref, bg_row_ref, bg_col_ref, cw_ref,
        ng_ref, wout_ref, g_ref, b_ref, o_ref, c_ref, n_ref, m_ref, cprev_ref, cbuf_ref, hn_ref))


def _mlstm_step(p_write, p_read, xn_ref, xc_ref, win_ref, wg_ref, wgt_ref, bg_row_ref, bg_col_ref, cw_ref,
                ng_ref, wout_ref, g_ref, b_ref, o_ref, c_ref, n_ref, m_ref, cprev_ref, cbuf_ref, hn_ref):
    L = C_CHUNK
    d = D_MODEL
    dh = C_HEAD_DIM
    nh = C_HEADS
    hl = CONV_HALO
    pw = 4 * d // nh

    xnb = xn_ref[...].astype(BF16)
    x = xc_ref[...]
    xb = x.astype(BF16)
    g_col = _dot(xb, wg_ref[...]) + bg_row_ref[...]
    g_row = _dot_nt(wgt_ref[...], xb) + bg_col_ref[...]

    cbuf_ref[0:hl, :] = cprev_ref[...]
    cbuf_ref[hl:hl + L, :] = p_read[:, 0:2 * d]
    cprev_ref[...] = p_read[L - hl:L, 0:2 * d]
    conv = cw_ref[C_CONV - 1:C_CONV, :] * p_read[:, 0:2 * d]
    for j in range(C_CONV - 2, -1, -1):
        off = hl - (C_CONV - 1) + j
        conv = conv + cw_ref[j:j + 1, :] * cbuf_ref[off:off + L, :]
    qk = conv * _sigmoid(conv)

    r = lax.broadcasted_iota(jnp.int32, (L, L), 0)
    cc = lax.broadcasted_iota(jnp.int32, (L, L), 1)
    lower = cc <= r

    for hd in range(nh):
        p_write[:, hd * pw:(hd + 1) * pw] = _dot(xnb, win_ref[:, hd * pw:(hd + 1) * pw])

        i_col = g_col[:, hd:hd + 1]
        i_row = g_row[hd:hd + 1, :]
        lf_col = _log_sigmoid(g_col[:, nh + hd:nh + hd + 1])
        lf_row = _log_sigmoid(g_row[nh + hd:nh + hd + 1, :])
        b_col = jnp.sum(jnp.where(lower, lf_row, 0.0), axis=1, keepdims=True)
        b_row = jnp.sum(jnp.where(r <= cc, lf_col, 0.0), axis=0, keepdims=True)
        b_last = jnp.sum(lf_row, axis=1, keepdims=True)

        q_h = qk[:, hd * dh:(hd + 1) * dh]
        k_h = qk[:, d + hd * dh:d + (hd + 1) * dh] * (dh ** -0.5)
        qb = q_h.astype(BF16)
        kb = k_h.astype(BF16)
        vb = p_read[:, 2 * d + hd * dh:2 * d + (hd + 1) * dh].astype(BF16)
        c_st = c_ref[hd]
        n_st = n_ref[hd]
        m_prev = m_ref[hd]

        d_intra = jnp.where(lower, b_col - b_row + i_row, NEG)
        m_inter = b_col + m_prev
        m_t = jnp.maximum(m_inter, jnp.max(d_intra, axis=1, keepdims=True))
        w = jnp.exp(d_intra - m_t) * _dot_nt(qb, kb)
        s_inter = jnp.exp(m_inter - m_t)
        num = s_inter * _dot(qb, c_st.astype(BF16)) + _dot(w.astype(BF16), vb)
        den = s_inter * jnp.sum(q_h * n_st, axis=1, keepdims=True) + jnp.sum(w, axis=1, keepdims=True)
        ht = num / jnp.maximum(jnp.abs(den), jnp.exp(-m_t))

        gg_col = b_last - b_col + i_col
        gg_row = b_last - b_row + i_row
        m_new = jnp.maximum(b_last + m_prev, jnp.max(gg_row, axis=1, keepdims=True))
        decay = jnp.exp(b_last + m_prev - m_new)
        kw = k_h * jnp.exp(gg_col - m_new)
        c_ref[hd] = decay * c_st + _dot(jnp.transpose(kw).astype(BF16), vb)
        n_ref[hd] = decay * n_st + jnp.sum(kw, axis=0, keepdims=True)
        m_ref[hd] = m_new

        hc = _sigmoid(p_read[:, 3 * d + hd * dh:3 * d + (hd + 1) * dh]) * ht
        mu = jnp.mean(hc, axis=1, keepdims=True)
        hcc = hc - mu
        var = jnp.mean(hcc * hcc, axis=1, keepdims=True)
        hn = hcc * lax.rsqrt(var + LN_EPS) * ng_ref[:, hd * dh:(hd + 1) * dh]
        hn_ref[:, hd * dh:(hd + 1) * dh] = hn.astype(BF16)

    y = _dot(hn_ref[...], wout_ref[...])
    o_ref[...] = _layer_norm(ALPHA * x + y, g_ref[...], b_ref[...])


def _mlstm_layer(x, w_in, b_gates, conv_w, norm_g, w_out, g, b, batch, seq):
    t = x.shape[0]
    L = C_CHUNK
    nc = seq // L
    d = D_MODEL
    nh = C_HEADS
    assert seq % L == 0
    w_gate = w_in[:, 4 * d:]
    wg = jnp.pad(w_gate, ((0, 0), (0, GATE_PAD - 2 * nh)))
    wgt = jnp.pad(w_gate.T, ((0, 16 - 2 * nh), (0, 0)))
    bg_row = jnp.pad(b_gates, (0, GATE_PAD - 2 * nh))[None, :].astype(F32)
    bg_col = jnp.pad(b_gates, (0, 16 - 2 * nh))[:, None].astype(F32)
    return pl.pallas_call(
        _mlstm_kernel,
        grid=(batch, nc + 1),
        in_specs=[
            pl.BlockSpec((L, d), lambda bb, c: (bb * nc + jnp.minimum(c, nc - 1), 0)),
            pl.BlockSpec((L, d), lambda bb, c: (bb * nc + jnp.maximum(c - 1, 0), 0)),
            pl.BlockSpec((d, 4 * d), lambda bb, c: (0, 0), pipeline_mode=pl.Buffered(1)),
            _const_spec((d, GATE_PAD)),
            _const_spec((16, d)),
            _const_spec((1, GATE_PAD)),
            _const_spec((16, 1)),
            _const_spec((C_CONV, 2 * d)),
            _const_spec((1, d)),
            _const_spec((d, d)),
            _const_spec((1, d)),
            _const_spec((1, d)),
        ],
        out_specs=pl.BlockSpec((L, d), lambda bb, c: (bb * nc + jnp.maximum(c - 1, 0), 0)),
        out_shape=jax.ShapeDtypeStruct((t, d), F32),
        scratch_shapes=[
            pltpu.VMEM((nh, C_HEAD_DIM, C_HEAD_DIM), F32),
            pltpu.VMEM((nh, 1, C_HEAD_DIM), F32),
            pltpu.VMEM((nh, 1, 1), F32),
            pltpu.VMEM((CONV_HALO, 2 * d), F32),
            pltpu.VMEM((CONV_HALO + L, 2 * d), F32),
            pltpu.VMEM((L, d), BF16),
            pltpu.VMEM((L, 4 * d), F32),
            pltpu.VMEM((L, 4 * d), F32),
        ],
        compiler_params=_params("arbitrary", "arbitrary"),
        name="mlstm_layer",
    )(x, x, w_in, wg.astype(BF16), wgt.astype(BF16), bg_row, bg_col, conv_w.astype(F32),
      norm_g[None, :].astype(F32), w_out, g, b)


def kernel(x, rel_bias, ln_g, ln_b, ffn_w_gu, ffn_w_down, a_w_in, a_w_out, b_w_in, b_w_group, b_scale, b_w_out,
           c_w_in, c_b_gates, c_conv_w, c_norm_g, c_w_out):
    batch, seq, d = x.shape
    h = x.reshape(batch * seq, d)
    bf = lambda w: w.astype(BF16)
    for i in range(DEPTH):
        lg = lambda s: ln_g[i, s][None, :]
        lb = lambda s: ln_b[i, s][None, :]
        h = _ffn(h, ffn_w_gu, ffn_w_down, i, 0, lg(0), lb(0))
        kind, j = i % N_MIXERS, i // N_MIXERS
        pre = None
        if kind == 0:
            pre = (_moba_mixer(h, a_w_in[j], rel_bias, batch, seq), bf(a_w_out[j]), lg(1), lb(1))
        elif kind == 1:
            h = _pool_layer(h, bf(b_w_in[j]), bf(b_w_group[j]), b_scale[j][None, :], bf(b_w_out[j]),
                            lg(1), lb(1), seq)
        else:
            h = _mlstm_layer(h, bf(c_w_in[j]), c_b_gates[j], c_conv_w[j], c_norm_g[j], bf(c_w_out[j]),
                             lg(1), lb(1), batch, seq)
        h = _ffn(h, ffn_w_gu, ffn_w_down, i, 1, lg(2), lb(2), pre)
    return h.reshape(batch, seq, d)
```

```python
import functools
import math

import jax
import jax.numpy as jnp
from jax import lax
from jax.experimental import pallas as pl
from jax.experimental.pallas import tpu as pltpu

F32 = jnp.float32
BF16 = jnp.bfloat16

D_MODEL = 1024
DEPTH = 4
N_MIXERS = 3
D_FF = 2816
LN_EPS = 1e-5
ALPHA = (2 * DEPTH) ** 0.25
A_HEADS = 8
A_HEAD_DIM = D_MODEL // A_HEADS
MOBA_BLOCK = 256
MOBA_TOPK = 3
REL_BUCKETS = 32
REL_MAX_EXACT = REL_BUCKETS // 2
REL_MAX_DIST = 128
POOL_WINDOWS = (2, 4, 8, 16)
POOL_GROUP = D_MODEL // len(POOL_WINDOWS)
POOL_HALO = 16
C_HEADS = 4
C_HEAD_DIM = D_MODEL // C_HEADS
C_CONV = 4
C_CHUNK = 256
CONV_HALO = 8
GATE_PAD = 128

NEG = -1e30
LOG2E = math.log2(math.e)
MOBA_GROUP = 4
MOBA_HEADS_PER_STEP = 2
MOBA_EXP_ROWS = 64
MOBA_VT_ROWS = 128 + 16
V7X_VMEM_LIMIT = 56 * 1024 * 1024
FFN_CHUNK = 256
FFN_ROW_TILE = 1024
FFN_SUBTILE_ROWS = 256
FFN_STAGE_COLS = 256
FFN_STAGE_ROWS = 128
FFN_STAGE_SLOTS = 4
ROW_TILE = 512

NT_DIMS = (((1,), (1,)), ((), ()))


def _params(*sem):
    return pltpu.CompilerParams(dimension_semantics=sem, vmem_limit_bytes=V7X_VMEM_LIMIT)


def _const_spec(shape):
    nd = len(shape)
    return pl.BlockSpec(shape, lambda *_: (0,) * nd, pipeline_mode=pl.Buffered(1))


def _layer_norm(z, g, b):
    mu = jnp.mean(z, axis=-1, keepdims=True)
    zc = z - mu
    var = jnp.mean(zc * zc, axis=-1, keepdims=True)
    return zc * lax.rsqrt(var + LN_EPS) * g + b


def _sigmoid(x):
    return 1.0 / (1.0 + jnp.exp(-x))


def _dot(a, b):
    return jnp.dot(a, b, preferred_element_type=F32)


def _dot_nt(a, b):
    return lax.dot_general(a, b, NT_DIMS, preferred_element_type=F32)


def _stage_as_bf16(chunks, stage_ref, sem_ref):
    slots = stage_ref.shape[0]

    def copy(c):
        return pltpu.make_async_copy(chunks[c][0], stage_ref.at[c % slots], sem_ref.at[c % slots])

    for c in range(min(slots, len(chunks))):
        copy(c).start()
    for c in range(len(chunks)):
        copy(c).wait()
        dst = chunks[c][1]
        dst[...] = stage_ref[c % slots].astype(BF16)
        if c + slots < len(chunks):
            copy(c + slots).start()


def _ffn_kernel(*refs, has_pre, layer, slot):
    x_ref = refs[0]
    (wgu_hbm, wd_hbm, g_ref, b_ref, o_ref,
     xb_ref, h_ref, wgu_ref, wd_ref, stage_gu_ref, stage_d_ref, sem_gu_ref, sem_d_ref) = refs[1 + 4 * has_pre:]

    @pl.when(pl.program_id(0) == 0)
    def _():
        wc, wr = FFN_STAGE_COLS, FFN_STAGE_ROWS
        gu = wgu_hbm.at[layer, slot]
        dn = wd_hbm.at[layer, slot]
        _stage_as_bf16([(gu.at[:, pl.ds(c * wc, wc)], wgu_ref.at[:, pl.ds(c * wc, wc)])
                        for c in range(2 * D_FF // wc)], stage_gu_ref, sem_gu_ref)
        _stage_as_bf16([(dn.at[pl.ds(r * wr, wr), :], wd_ref.at[pl.ds(r * wr, wr), :])
                        for r in range(D_FF // wr)], stage_d_ref, sem_d_ref)

    sm = FFN_SUBTILE_ROWS
    nsub = x_ref.shape[0] // sm
    if has_pre:
        a_ref, wa_ref, ga_ref, ba_ref = refs[1:5]
        for s in range(nsub):
            rows = slice(s * sm, (s + 1) * sm)
            o_ref[rows, :] = _layer_norm(ALPHA * x_ref[rows, :] + _dot(a_ref[rows, :], wa_ref[...]),
                                         ga_ref[...], ba_ref[...])
    for s in range(nsub):
        rows = slice(s * sm, (s + 1) * sm)
        buf = s % 2
        x = o_ref[rows, :] if has_pre else x_ref[rows, :]
        xb_ref[buf] = x.astype(BF16)
        for c in range(D_FF // FFN_CHUNK):
            lo = c * FFN_CHUNK
            xb = xb_ref[buf]
            gate = _dot(xb, wgu_ref[:, lo:lo + FFN_CHUNK])
            up = _dot(xb, wgu_ref[:, D_FF + lo:D_FF + lo + FFN_CHUNK])
            h_ref[buf, :, lo:lo + FFN_CHUNK] = (gate * _sigmoid(gate) * up).astype(BF16)
        y = _dot(h_ref[buf], wd_ref[...])
        o_ref[rows, :] = _layer_norm(ALPHA * x + 0.5 * y, g_ref[...], b_ref[...])


def _ffn(x, w_gu_all, w_down_all, layer, slot, g, b, pre=None):
    t = x.shape[0]
    tm = FFN_ROW_TILE
    sm = FFN_SUBTILE_ROWS
    assert (2 * D_FF) % FFN_STAGE_COLS == 0 and D_FF % FFN_STAGE_ROWS == 0
    row = lambda i: (i, 0)
    vec = _const_spec((1, D_MODEL))
    hbm = pl.BlockSpec(memory_space=pl.ANY)
    in_specs = [pl.BlockSpec((tm, D_MODEL), row)]
    args = [x]
    if pre is not None:
        in_specs += [pl.BlockSpec((tm, D_MODEL), row), _const_spec((D_MODEL, D_MODEL)), vec, vec]
        args += list(pre)
    in_specs += [hbm, hbm, vec, vec]
    args += [w_gu_all, w_down_all, g, b]
    return pl.pallas_call(
        functools.partial(_ffn_kernel, has_pre=pre is not None, layer=layer, slot=slot),
        grid=(t // tm,),
        in_specs=in_specs,
        out_specs=pl.BlockSpec((tm, D_MODEL), row),
        out_shape=jax.ShapeDtypeStruct((t, D_MODEL), F32),
        scratch_shapes=[
            pltpu.VMEM((2, sm, D_MODEL), BF16),
            pltpu.VMEM((2, sm, D_FF), BF16),
            pltpu.VMEM((D_MODEL, 2 * D_FF), BF16),
            pltpu.VMEM((D_FF, D_MODEL), BF16),
            pltpu.VMEM((FFN_STAGE_SLOTS, D_MODEL, FFN_STAGE_COLS), F32),
            pltpu.VMEM((FFN_STAGE_SLOTS, FFN_STAGE_ROWS, D_MODEL), F32),
            pltpu.SemaphoreType.DMA((FFN_STAGE_SLOTS,)),
            pltpu.SemaphoreType.DMA((FFN_STAGE_SLOTS,)),
        ],
        compiler_params=_params("arbitrary"),
        name="ffn_pre" if pre is not None else "ffn",
    )(*args)


def _t5_bucket(dist):
    n = jnp.maximum(dist, 0)
    is_small = n < REL_MAX_EXACT
    nf = jnp.maximum(n, 1).astype(F32)
    large = REL_MAX_EXACT + (jnp.log(nf / REL_MAX_EXACT) / math.log(REL_MAX_DIST / REL_MAX_EXACT)
                             * (REL_BUCKETS - REL_MAX_EXACT)).astype(jnp.int32)
    large = jnp.minimum(large, REL_BUCKETS - 1)
    return jnp.where(is_small, n, large)


def _moba_bias_tables(rel_bias):
    blk = MOBA_BLOCK
    nh = rel_bias.shape[1]
    far = rel_bias[REL_BUCKETS - 1][:, None]

    def by_distance(dist):
        onehot = _t5_bucket(dist)[:, None] == jnp.arange(REL_BUCKETS)
        picked = jnp.sum(jnp.where(onehot[None], rel_bias.T[:, None, :], 0.0), axis=-1)
        return (picked - far) * LOG2E

    def toeplitz(v):
        flat = jnp.broadcast_to(v[:, None, :], (nh, blk, 2 * blk)).reshape(nh, 2 * blk * blk)
        skew = flat[:, blk - 1:blk - 1 + blk * (2 * blk - 1)].reshape(nh, blk, 2 * blk - 1)
        return skew[:, :, :blk]

    d = jnp.arange(2 * blk) - (blk - 1)
    own = toeplitz(jnp.where(d[None] >= 0, by_distance(d), NEG))
    adj = toeplitz(by_distance(d + blk))
    return jnp.stack([own, adj, jnp.zeros_like(adj)], axis=1).astype(F32)


def _moba_qkv_kernel(x_ref, wqk_ref, wvt_ref, qk_ref, vt_ref):
    xb = x_ref[...].astype(BF16)
    qk_ref[...] = _dot(xb, wqk_ref[...]).astype(BF16)
    vt = _dot_nt(wvt_ref[...], xb).astype(BF16)
    dh, rows = A_HEAD_DIM, MOBA_VT_ROWS
    for h in range(A_HEADS):
        vt_ref[h * rows:h * rows + dh, :] = vt[h * dh:(h + 1) * dh, :]
        vt_ref[h * rows + dh:(h + 1) * rows, :] = jnp.ones((rows - dh, vt.shape[1]), BF16)


def _moba_qkv(x, w_qk, w_vt, batch, seq):
    t = x.shape[0]
    tm = ROW_TILE
    tps = seq // tm
    return pl.pallas_call(
        _moba_qkv_kernel,
        grid=(t // tm,),
        in_specs=[
            pl.BlockSpec((tm, D_MODEL), lambda i: (i, 0)),
            _const_spec((D_MODEL, 2 * D_MODEL)),
            _const_spec((D_MODEL, D_MODEL)),
        ],
        out_specs=[
            pl.BlockSpec((tm, 2 * D_MODEL), lambda i: (i, 0)),
            pl.BlockSpec((None, A_HEADS * MOBA_VT_ROWS, tm), lambda i: (i // tps, 0, i % tps)),
        ],
        out_shape=[
            jax.ShapeDtypeStruct((t, 2 * D_MODEL), BF16),
            jax.ShapeDtypeStruct((batch, A_HEADS * MOBA_VT_ROWS, seq), BF16),
        ],
        compiler_params=_params("parallel"),
        name="moba_qkv",
    )(x, w_qk, w_vt)


def _moba_kernel(q_ref, qn_ref, k_ref, vt_ref, tab_ref, o_ref, kmean_ref, sel_ref, t_ref, p_ref, m_ref, *, nb):
    blk = MOBA_BLOCK
    dh = A_HEAD_DIM
    gb = MOBA_GROUP
    rows = MOBA_VT_ROWS
    sub = MOBA_EXP_ROWS
    heads = range(MOBA_HEADS_PER_STEP)
    s = pl.program_id(2)

    @pl.when(s == 0)
    def _():
        for hh in heads:
            for j in range(nb):
                kb = k_ref[j * blk:(j + 1) * blk, hh * dh:(hh + 1) * dh].astype(F32)
                kmean_ref[hh, j:j + 1, :] = jnp.mean(kb, axis=0, keepdims=True)
        o_ref[...] = jnp.zeros_like(o_ref)
        own = lax.broadcasted_iota(jnp.int32, (nb, blk), 0) == 0
        for hh in heads:
            sel_ref[hh] = jnp.where(own, 1.0, 0.0)

    def select_next():
        i = jnp.minimum(s + 1, nb - 1)
        jidx = lax.broadcasted_iota(jnp.int32, (nb, blk), 0)
        for hh in heads:
            q = qn_ref[:, hh * dh:(hh + 1) * dh]
            gate = _dot_nt(kmean_ref[hh].astype(BF16), q)
            cnt = jnp.zeros((nb, blk), F32)
            for jp in range(nb - 1):
                row = gate[jp:jp + 1, :]
                beats = (row > gate) | ((row == gate) & (jp < jidx))
                cnt = cnt + jnp.where(beats & (jp < i), 1.0, 0.0)
            chosen = ((cnt < MOBA_TOPK) & (jidx < i)) | (jidx == i)
            sel_ref[hh] = jnp.where(chosen, 1.0, 0.0)

    def step(ng_score, ng_finish):
        first_dynamic = (ng_score - 1) * gb - 1
        nk_finish = ng_finish * gb * blk
        if ng_finish:
            m_prev = [m_ref[hh] for hh in heads]
        m8 = [None] * len(heads)
        for j in range(max(ng_score, ng_finish) * gb):
            if j < ng_finish * gb:
                for hh in heads:
                    for r in range(j * blk, (j + 1) * blk, sub):
                        p_ref[hh, r:r + sub, :] = jnp.exp2(t_ref[hh, r:r + sub, :] - m_prev[hh]).astype(BF16)
            if j < ng_score * gb:
                for hh in heads:
                    t = _dot_nt(k_ref[j * blk:(j + 1) * blk, hh * dh:(hh + 1) * dh],
                                q_ref[:, hh * dh:(hh + 1) * dh])
                    if j >= first_dynamic:
                        t = t + tab_ref[hh, jnp.clip(s - j, 0, 2)]
                    t = jnp.where(sel_ref[hh, j:j + 1, :] > 0.5, t, NEG)
                    t_ref[hh, j * blk:(j + 1) * blk, :] = t
                    mb = jnp.max(t.reshape(blk // 8, 8, blk), axis=0)
                    m8[hh] = mb if j == 0 else jnp.maximum(m8[hh], mb)
        if ng_score:
            for hh in heads:
                m_ref[hh] = jnp.max(m8[hh], axis=0, keepdims=True)
        select_next()
        if ng_finish:
            for hh in heads:
                o_aug = _dot(vt_ref[hh * rows:(hh + 1) * rows, 0:nk_finish], p_ref[hh, 0:nk_finish, :])
                o_t = o_aug[0:dh, :] / o_aug[dh:dh + 1, :]
                o_ref[:, hh * dh:(hh + 1) * dh] = jnp.transpose(o_t).astype(o_ref.dtype)

    ng_score = jnp.where(s < nb, s // gb + 1, 0)
    ng_finish = jnp.where(s >= 1, (s - 1) // gb + 1, 0)
    combos = sorted({(q // gb + 1 if q < nb else 0, (q - 1) // gb + 1 if q >= 1 else 0) for q in range(nb + 1)})
    for a, b in combos:
        pl.when((ng_score == a) & (ng_finish == b))(functools.partial(step, a, b))


def _moba_attention(qk, vt, tab, batch, seq):
    blk = MOBA_BLOCK
    nb = seq // blk
    hp = MOBA_HEADS_PER_STEP
    w = hp * A_HEAD_DIM
    ngrp = A_HEADS // hp
    assert seq % blk == 0 and nb % MOBA_GROUP == 0 and A_HEADS % hp == 0
    kern = functools.partial(_moba_kernel, nb=nb)
    return pl.pallas_call(
        kern,
        grid=(batch, ngrp, nb + 1),
        in_specs=[
            pl.BlockSpec((blk, w), lambda b, h, s: (b * nb + jnp.minimum(s, nb - 1), h)),
            pl.BlockSpec((blk, w), lambda b, h, s: (b * nb + jnp.minimum(s + 1, nb - 1), h)),
            pl.BlockSpec((seq, w), lambda b, h, s: (b, ngrp + h)),
            pl.BlockSpec((None, hp * MOBA_VT_ROWS, seq), lambda b, h, s: (b, h, 0)),
            pl.BlockSpec((hp, 3, blk, blk), lambda b, h, s: (h, 0, 0, 0)),
        ],
        out_specs=pl.BlockSpec((blk, w), lambda b, h, s: (b * nb + jnp.maximum(s - 1, 0), h)),
        out_shape=jax.ShapeDtypeStruct((batch * seq, D_MODEL), BF16),
        scratch_shapes=[
            pltpu.VMEM((hp, nb, A_HEAD_DIM), F32),
            pltpu.VMEM((hp, nb, blk), F32),
            pltpu.VMEM((hp, seq, blk), F32),
            pltpu.VMEM((hp, seq, blk), BF16),
            pltpu.VMEM((hp, 1, blk), F32),
        ],
        compiler_params=_params("parallel", "parallel", "arbitrary"),
        name="moba_attn",
    )(qk, qk, qk, vt, tab)


def _moba_mixer(x, w_in, rel_bias, batch, seq):
    assert REL_MAX_DIST <= MOBA_BLOCK
    d = D_MODEL
    c1 = (A_HEAD_DIM ** -0.5) * LOG2E
    w_qk = jnp.concatenate([w_in[:, :d] * c1, w_in[:, d:2 * d]], axis=1).astype(BF16)
    w_vt = w_in[:, 2 * d:].T.astype(BF16)
    qk, vt = _moba_qkv(x, w_qk, w_vt, batch, seq)
    return _moba_attention(qk, vt, _moba_bias_tables(rel_bias), batch, seq)


def _pool_kernel(x_ref, halo_ref, win_ref, wgrp_ref, scale_ref, wout_ref, g_ref, b_ref, o_ref,
                 ubuf_ref, ybuf_ref, *, tiles_per_seq):
    tm = x_ref.shape[0]
    hl = POOL_HALO
    ti = pl.program_id(0) % tiles_per_seq
    x = x_ref[...]
    u_halo = _dot(halo_ref[...].astype(BF16), win_ref[...])
    ubuf_ref[0:hl, :] = jnp.where(ti == 0, 0.0, u_halo)
    ubuf_ref[hl:hl + tm, :] = _dot(x.astype(BF16), win_ref[...])
    pos = ti * tm + lax.broadcasted_iota(jnp.int32, (tm, POOL_GROUP), 0)
    for gi, w in enumerate(POOL_WINDOWS):
        lo = gi * POOL_GROUP
        u = ubuf_ref[hl:hl + tm, lo:lo + POOL_GROUP]
        acc = ubuf_ref[:, lo:lo + POOL_GROUP]
        d = 1
        while d < w:
            acc = acc + pltpu.roll(acc, d, 0)
            d *= 2
        ws = acc[hl:hl + tm, :]
        cnt = jnp.minimum(pos + 1, w).astype(F32)
        pooled = ws / cnt - u
        yg = _dot(pooled.astype(BF16), wgrp_ref[gi]) * scale_ref[:, lo:lo + POOL_GROUP]
        ybuf_ref[:, lo:lo + POOL_GROUP] = yg.astype(BF16)
    y = _dot(ybuf_ref[...], wout_ref[...])
    o_ref[...] = _layer_norm(ALPHA * x + y, g_ref[...], b_ref[...])


def _pool_layer(x, w_in, w_group, scale, w_out, g, b, seq):
    t = x.shape[0]
    tm = ROW_TILE
    hl = POOL_HALO
    assert seq % tm == 0 and tm % hl == 0 and max(POOL_WINDOWS) <= hl
    kern = functools.partial(_pool_kernel, tiles_per_seq=seq // tm)
    ng = len(POOL_WINDOWS)
    return pl.pallas_call(
        kern,
        grid=(t // tm,),
        in_specs=[
            pl.BlockSpec((tm, D_MODEL), lambda i: (i, 0)),
            pl.BlockSpec((hl, D_MODEL), lambda i: (jnp.maximum(i * (tm // hl) - 1, 0), 0)),
            _const_spec((D_MODEL, D_MODEL)),
            _const_spec((ng, POOL_GROUP, POOL_GROUP)),
            _const_spec((1, D_MODEL)),
            _const_spec((D_MODEL, D_MODEL)),
            _const_spec((1, D_MODEL)),
            _const_spec((1, D_MODEL)),
        ],
        out_specs=pl.BlockSpec((tm, D_MODEL), lambda i: (i, 0)),
        out_shape=jax.ShapeDtypeStruct((t, D_MODEL), F32),
        scratch_shapes=[pltpu.VMEM((tm + hl, D_MODEL), F32), pltpu.VMEM((tm, D_MODEL), BF16)],
        compiler_params=_params("parallel"),
        name="pool_layer",
    )(x, x, w_in, w_group, scale, w_out, g, b)


def _log_sigmoid(x):
    return jnp.minimum(x, 0.0) - jnp.log1p(jnp.exp(-jnp.abs(x)))


def _mlstm_kernel(xn_ref, xc_ref, win_ref, wg_ref, wgt_ref, bg_row_ref, bg_col_ref, cw_ref,
                  ng_ref, wout_ref, g_ref, b_ref, o_ref,
                  c_ref, n_ref, m_ref, cprev_ref, cbuf_ref, hn_ref, pa_ref, pb_ref):
    c = pl.program_id(1)

    @pl.when(c == 0)
    def _():
        pb_ref[...] = jnp.zeros_like(pb_ref)

    @pl.when(c <= 1)
    def _():
        c_ref[...] = jnp.zeros_like(c_ref)
        n_ref[...] = jnp.zeros_like(n_ref)
        m_ref[...] = jnp.zeros_like(m_ref)
        cprev_ref[...] = jnp.zeros_like(cprev_ref)

    pl.when(c % 2 == 0)(functools.partial(
        _mlstm_step, pa_ref, pb_ref, xn_ref, xc_ref, win_ref, wg_ref, wgt_ref, bg_row_ref, bg_col_ref, cw_ref,
        ng_ref, wout_ref, g_ref, b_ref, o_ref, c_ref, n_ref, m_ref, cprev_ref, cbuf_ref, hn_ref))
    pl.when(c % 2 == 1)(functools.partial(
        _mlstm_step, pb_ref, pa_ref, xn_ref, xc_ref, win_ref, wg_ref, wgt_ref, bg_row_ref, bg_col_ref, cw_ref,
        ng_ref, wout_ref, g_ref, b_ref, o_ref, c_ref, n_ref, m_ref, cprev_ref, cbuf_ref, hn_ref))


def _mlstm_step(p_write, p_read, xn_ref, xc_ref, win_ref, wg_ref, wgt_ref, bg_row_ref, bg_col_ref, cw_ref,
                ng_ref, wout_ref, g_ref, b_ref, o_ref, c_ref, n_ref, m_ref, cprev_ref, cbuf_ref, hn_ref):
    L = C_CHUNK
    d = D_MODEL
    dh = C_HEAD_DIM
    nh = C_HEADS
    hl = CONV_HALO
    pw = 4 * d // nh

    xnb = xn_ref[...].astype(BF16)
    x = xc_ref[...]
    xb = x.astype(BF16)
    g_col = _dot(xb, wg_ref[...]) + bg_row_ref[...]
    g_row = _dot_nt(wgt_ref[...], xb) + bg_col_ref[...]

    cbuf_ref[0:hl, :] = cprev_ref[...]
    cbuf_ref[hl:hl + L, :] = p_read[:, 0:2 * d]
    cprev_ref[...] = p_read[L - hl:L, 0:2 * d]
    conv = cw_ref[C_CONV - 1:C_CONV, :] * p_read[:, 0:2 * d]
    for j in range(C_CONV - 2, -1, -1):
        off = hl - (C_CONV - 1) + j
        conv = conv + cw_ref[j:j + 1, :] * cbuf_ref[off:off + L, :]
    qk = conv * _sigmoid(conv)

    r = lax.broadcasted_iota(jnp.int32, (L, L), 0)
    cc = lax.broadcasted_iota(jnp.int32, (L, L), 1)
    lower = cc <= r

    for hd in range(nh):
        p_write[:, hd * pw:(hd + 1) * pw] = _dot(xnb, win_ref[:, hd * pw:(hd + 1) * pw])

        i_col = g_col[:, hd:hd + 1]
        i_row = g_row[hd:hd + 1, :]
        lf_col = _log_sigmoid(g_col[:, nh + hd:nh + hd + 1])
        lf_row = _log_sigmoid(g_row[nh + hd:nh + hd + 1, :])
        b_col = jnp.sum(jnp.where(lower, lf_row, 0.0), axis=1, keepdims=True)
        b_row = jnp.sum(jnp.where(r <= cc, lf_col, 0.0), axis=0, keepdims=True)
        b_last = jnp.sum(lf_row, axis=1, keepdims=True)

        q_h = qk[:, hd * dh:(hd + 1) * dh]
        k_h = qk[:, d + hd * dh:d + (hd + 1) * dh] * (dh ** -0.5)
        qb = q_h.astype(BF16)
        kb = k_h.astype(BF16)
        vb = p_read[:, 2 * d + hd * dh:2 * d + (hd + 1) * dh].astype(BF16)
        c_st = c_ref[hd]
        n_st = n_ref[hd]
        m_prev = m_ref[hd]

        d_intra = jnp.where(lower, b_col - b_row + i_row, NEG)
        m_inter = b_col + m_prev
        m_t = jnp.maximum(m_inter, jnp.max(d_intra, axis=1, keepdims=True))
        w = jnp.exp(d_intra - m_t) * _dot_nt(qb, kb)
        s_inter = jnp.exp(m_inter - m_t)
        num = s_inter * _dot(qb, c_st.astype(BF16)) + _dot(w.astype(BF16), vb)
        den = s_inter * jnp.sum(q_h * n_st, axis=1, keepdims=True) + jnp.sum(w, axis=1, keepdims=True)
        ht = num / jnp.maximum(jnp.abs(den), jnp.exp(-m_t))

        gg_col = b_last - b_col + i_col
        gg_row = b_last - b_row + i_row
        m_new = jnp.maximum(b_last + m_prev, jnp.max(gg_row, axis=1, keepdims=True))
        decay = jnp.exp(b_last + m_prev - m_new)
        kw = k_h * jnp.exp(gg_col - m_new)
        c_ref[hd] = decay * c_st + _dot(jnp.transpose(kw).astype(BF16), vb)
        n_ref[hd] = decay * n_st + jnp.sum(kw, axis=0, keepdims=True)
        m_ref[hd] = m_new

        hc = _sigmoid(p_read[:, 3 * d + hd * dh:3 * d + (hd + 1) * dh]) * ht
        mu = jnp.mean(hc, axis=1, keepdims=True)
        hcc = hc - mu
        var = jnp.mean(hcc * hcc, axis=1, keepdims=True)
        hn = hcc * lax.rsqrt(var + LN_EPS) * ng_ref[:, hd * dh:(hd + 1) * dh]
        hn_ref[:, hd * dh:(hd + 1) * dh] = hn.astype(BF16)

    y = _dot(hn_ref[...], wout_ref[...])
    o_ref[...] = _layer_norm(ALPHA * x + y, g_ref[...], b_ref[...])


def _mlstm_layer(x, w_in, b_gates, conv_w, norm_g, w_out, g, b, batch, seq):
    t = x.shape[0]
    L = C_CHUNK
    nc = seq // L
    d = D_MODEL
    nh = C_HEADS
    assert seq % L == 0
    w_gate = w_in[:, 4 * d:]
    wg = jnp.pad(w_gate, ((0, 0), (0, GATE_PAD - 2 * nh)))
    wgt = jnp.pad(w_gate.T, ((0, 16 - 2 * nh), (0, 0)))
    bg_row = jnp.pad(b_gates, (0, GATE_PAD - 2 * nh))[None, :].astype(F32)
    bg_col = jnp.pad(b_gates, (0, 16 - 2 * nh))[:, None].astype(F32)
    return pl.pallas_call(
        _mlstm_kernel,
        grid=(batch, nc + 1),
        in_specs=[
            pl.BlockSpec((L, d), lambda bb, c: (bb * nc + jnp.minimum(c, nc - 1), 0)),
            pl.BlockSpec((L, d), lambda bb, c: (bb * nc + jnp.maximum(c - 1, 0), 0)),
            pl.BlockSpec((d, 4 * d), lambda bb, c: (0, 0), pipeline_mode=pl.Buffered(1)),
            _const_spec((d, GATE_PAD)),
            _const_spec((16, d)),
            _const_spec((1, GATE_PAD)),
            _const_spec((16, 1)),
            _const_spec((C_CONV, 2 * d)),
            _const_spec((1, d)),
            _const_spec((d, d)),
            _const_spec((1, d)),
            _const_spec((1, d)),
        ],
        out_specs=pl.BlockSpec((L, d), lambda bb, c: (bb * nc + jnp.maximum(c - 1, 0), 0)),
        out_shape=jax.ShapeDtypeStruct((t, d), F32),
        scratch_shapes=[
            pltpu.VMEM((nh, C_HEAD_DIM, C_HEAD_DIM), F32),
            pltpu.VMEM((nh, 1, C_HEAD_DIM), F32),
            pltpu.VMEM((nh, 1, 1), F32),
            pltpu.VMEM((CONV_HALO, 2 * d), F32),
            pltpu.VMEM((CONV_HALO + L, 2 * d), F32),
            pltpu.VMEM((L, d), BF16),
            pltpu.VMEM((L, 4 * d), F32),
            pltpu.VMEM((L, 4 * d), F32),
        ],
        compiler_params=_params("arbitrary", "arbitrary"),
        name="mlstm_layer",
    )(x, x, w_in, wg.astype(BF16), wgt.astype(BF16), bg_row, bg_col, conv_w.astype(F32),
      norm_g[None, :].astype(F32), w_out, g, b)


def kernel(x, rel_bias, ln_g, ln_b, ffn_w_gu, ffn_w_down, a_w_in, a_w_out, b_w_in, b_w_group, b_scale, b_w_out,
           c_w_in, c_b_gates, c_conv_w, c_norm_g, c_w_out):
    batch, seq, d = x.shape
    h = x.reshape(batch * seq, d)
    bf = lambda w: w.astype(BF16)
    for i in range(DEPTH):
        lg = lambda s: ln_g[i, s][None, :]
        lb = lambda s: ln_b[i, s][None, :]
        h = _ffn(h, ffn_w_gu, ffn_w_down, i, 0, lg(0), lb(0))
        kind, j = i % N_MIXERS, i // N_MIXERS
        pre = None
        if kind == 0:
            pre = (_moba_mixer(h, a_w_in[j], rel_bias, batch, seq), bf(a_w_out[j]), lg(1), lb(1))
        elif kind == 1:
            h = _pool_layer(h, bf(b_w_in[j]), bf(b_w_group[j]), b_scale[j][None, :], bf(b_w_out[j]),
                            lg(1), lb(1), seq)
        else:
            h = _mlstm_layer(h, bf(c_w_in[j]), c_b_gates[j], c_conv_w[j], c_norm_g[j], bf(c_w_out[j]),
                             lg(1), lb(1), batch, seq)
        h = _ffn(h, ffn_w_gu, ffn_w_down, i, 1, lg(2), lb(2), pre)
    return h.reshape(batch, seq, d)
```

```python
import functools
import math

import jax
import jax.numpy as jnp
from jax import lax
from jax.experimental import pallas as pl
from jax.experimental.pallas import tpu as pltpu

F32 = jnp.float32
BF16 = jnp.bfloat16

D_MODEL = 1024
DEPTH = 4
N_MIXERS = 3
D_FF = 2816
LN_EPS = 1e-5
ALPHA = (2 * DEPTH) ** 0.25
A_HEADS = 8
A_HEAD_DIM = D_MODEL // A_HEADS
MOBA_BLOCK = 256
MOBA_TOPK = 3
REL_BUCKETS = 32
REL_MAX_EXACT = REL_BUCKETS // 2
REL_MAX_DIST = 128
POOL_WINDOWS = (2, 4, 8, 16)
POOL_GROUP = D_MODEL // len(POOL_WINDOWS)
POOL_HALO = 16
C_HEADS = 4
C_HEAD_DIM = D_MODEL // C_HEADS
C_CONV = 4
C_CHUNK = 256
CONV_HALO = 8
GATE_PAD = 128

NEG = -1e30
LOG2E = math.log2(math.e)
MOBA_GROUP = 4
MOBA_HEADS_PER_STEP = 2
MOBA_EXP_ROWS = 64
MOBA_VT_ROWS = 128 + 16
V7X_VMEM_LIMIT = 56 * 1024 * 1024
FFN_CHUNK = 256
FFN_ROW_TILE = 1024
FFN_SUBTILE_ROWS = 256
FFN_STAGE_COLS = 256
FFN_STAGE_ROWS = 128
FFN_STAGE_SLOTS = 4
ROW_TILE = 1024

NT_DIMS = (((1,), (1,)), ((), ()))


def _params(*sem):
    return pltpu.CompilerParams(dimension_semantics=sem, vmem_limit_bytes=V7X_VMEM_LIMIT)


def _const_spec(shape):
    nd = len(shape)
    return pl.BlockSpec(shape, lambda *_: (0,) * nd, pipeline_mode=pl.Buffered(1))


def _layer_norm(z, g, b):
    mu = jnp.mean(z, axis=-1, keepdims=True)
    zc = z - mu
    var = jnp.mean(zc * zc, axis=-1, keepdims=True)
    return zc * lax.rsqrt(var + LN_EPS) * g + b


def _sigmoid(x):
    return 1.0 / (1.0 + jnp.exp(-x))


def _dot(a, b):
    return jnp.dot(a, b, preferred_element_type=F32)


def _dot_nt(a, b):
    return lax.dot_general(a, b, NT_DIMS, preferred_element_type=F32)


def _stage_as_bf16(chunks, stage_ref, sem_ref):
    slots = stage_ref.shape[0]

    def copy(c):
        return pltpu.make_async_copy(chunks[c][0], stage_ref.at[c % slots], sem_ref.at[c % slots])

    for c in range(min(slots, len(chunks))):
        copy(c).start()
    for c in range(len(chunks)):
        copy(c).wait()
        dst = chunks[c][1]
        dst[...] = stage_ref[c % slots].astype(BF16)
        if c + slots < len(chunks):
            copy(c + slots).start()


def _ffn_kernel(*refs, has_pre, layer, slot):
    x_ref = refs[0]
    (wgu_hbm, wd_hbm, g_ref, b_ref, o_ref,
     xb_ref, h_ref, wgu_ref, wd_ref, stage_gu_ref, stage_d_ref, sem_gu_ref, sem_d_ref) = refs[1 + 4 * has_pre:]

    @pl.when(pl.program_id(0) == 0)
    def _():
        wc, wr = FFN_STAGE_COLS, FFN_STAGE_ROWS
        gu = wgu_hbm.at[layer, slot]
        dn = wd_hbm.at[layer, slot]
        _stage_as_bf16([(gu.at[:, pl.ds(c * wc, wc)], wgu_ref.at[:, pl.ds(c * wc, wc)])
                        for c in range(2 * D_FF // wc)], stage_gu_ref, sem_gu_ref)
        _stage_as_bf16([(dn.at[pl.ds(r * wr, wr), :], wd_ref.at[pl.ds(r * wr, wr), :])
                        for r in range(D_FF // wr)], stage_d_ref, sem_d_ref)

    sm = FFN_SUBTILE_ROWS
    nsub = x_ref.shape[0] // sm
    if has_pre:
        a_ref, wa_ref, ga_ref, ba_ref = refs[1:5]
        for s in range(nsub):
            rows = slice(s * sm, (s + 1) * sm)
            o_ref[rows, :] = _layer_norm(ALPHA * x_ref[rows, :] + _dot(a_ref[rows, :], wa_ref[...]),
                                         ga_ref[...], ba_ref[...])
    for s in range(nsub):
        rows = slice(s * sm, (s + 1) * sm)
        buf = s % 2
        x = o_ref[rows, :] if has_pre else x_ref[rows, :]
        xb_ref[buf] = x.astype(BF16)
        for c in range(D_FF // FFN_CHUNK):
            lo = c * FFN_CHUNK
            xb = xb_ref[buf]
            gate = _dot(xb, wgu_ref[:, lo:lo + FFN_CHUNK])
            up = _dot(xb, wgu_ref[:, D_FF + lo:D_FF + lo + FFN_CHUNK])
            h_ref[buf, :, lo:lo + FFN_CHUNK] = (gate * _sigmoid(gate) * up).astype(BF16)
        y = _dot(h_ref[buf], wd_ref[...])
        o_ref[rows, :] = _layer_norm(ALPHA * x + 0.5 * y, g_ref[...], b_ref[...])


def _ffn(x, w_gu_all, w_down_all, layer, slot, g, b, pre=None):
    t = x.shape[0]
    tm = FFN_ROW_TILE
    sm = FFN_SUBTILE_ROWS
    assert (2 * D_FF) % FFN_STAGE_COLS == 0 and D_FF % FFN_STAGE_ROWS == 0
    row = lambda i: (i, 0)
    vec = _const_spec((1, D_MODEL))
    hbm = pl.BlockSpec(memory_space=pl.ANY)
    in_specs = [pl.BlockSpec((tm, D_MODEL), row)]
    args = [x]
    if pre is not None:
        in_specs += [pl.BlockSpec((tm, D_MODEL), row), _const_spec((D_MODEL, D_MODEL)), vec, vec]
        args += list(pre)
    in_specs += [hbm, hbm, vec, vec]
    args += [w_gu_all, w_down_all, g, b]
    return pl.pallas_call(
        functools.partial(_ffn_kernel, has_pre=pre is not None, layer=layer, slot=slot),
        grid=(t // tm,),
        in_specs=in_specs,
        out_specs=pl.BlockSpec((tm, D_MODEL), row),
        out_shape=jax.ShapeDtypeStruct((t, D_MODEL), F32),
        scratch_shapes=[
            pltpu.VMEM((2, sm, D_MODEL), BF16),
            pltpu.VMEM((2, sm, D_FF), BF16),
            pltpu.VMEM((D_MODEL, 2 * D_FF), BF16),
            pltpu.VMEM((D_FF, D_MODEL), BF16),
            pltpu.VMEM((FFN_STAGE_SLOTS, D_MODEL, FFN_STAGE_COLS), F32),
            pltpu.VMEM((FFN_STAGE_SLOTS, FFN_STAGE_ROWS, D_MODEL), F32),
            pltpu.SemaphoreType.DMA((FFN_STAGE_SLOTS,)),
            pltpu.SemaphoreType.DMA((FFN_STAGE_SLOTS,)),
        ],
        compiler_params=_params("arbitrary"),
        name="ffn_pre" if pre is not None else "ffn",
    )(*args)


def _t5_bucket(dist):
    n = jnp.maximum(dist, 0)
    is_small = n < REL_MAX_EXACT
    nf = jnp.maximum(n, 1).astype(F32)
    large = REL_MAX_EXACT + (jnp.log(nf / REL_MAX_EXACT) / math.log(REL_MAX_DIST / REL_MAX_EXACT)
                             * (REL_BUCKETS - REL_MAX_EXACT)).astype(jnp.int32)
    large = jnp.minimum(large, REL_BUCKETS - 1)
    return jnp.where(is_small, n, large)


def _moba_bias_tables(rel_bias):
    blk = MOBA_BLOCK
    nh = rel_bias.shape[1]
    far = rel_bias[REL_BUCKETS - 1][:, None]

    def by_distance(dist):
        onehot = _t5_bucket(dist)[:, None] == jnp.arange(REL_BUCKETS)
        picked = jnp.sum(jnp.where(onehot[None], rel_bias.T[:, None, :], 0.0), axis=-1)
        return (picked - far) * LOG2E

    def toeplitz(v):
        flat = jnp.broadcast_to(v[:, None, :], (nh, blk, 2 * blk)).reshape(nh, 2 * blk * blk)
        skew = flat[:, blk - 1:blk - 1 + blk * (2 * blk - 1)].reshape(nh, blk, 2 * blk - 1)
        return skew[:, :, :blk]

    d = jnp.arange(2 * blk) - (blk - 1)
    own = toeplitz(jnp.where(d[None] >= 0, by_distance(d), NEG))
    adj = toeplitz(by_distance(d + blk))
    return jnp.stack([own, adj, jnp.zeros_like(adj)], axis=1).astype(F32)


def _moba_qkv_kernel(x_ref, wqk_ref, wvt_ref, qk_ref, vt_ref):
    xb = x_ref[...].astype(BF16)
    qk_ref[...] = _dot(xb, wqk_ref[...]).astype(BF16)
    vt = _dot_nt(wvt_ref[...], xb).astype(BF16)
    dh, rows = A_HEAD_DIM, MOBA_VT_ROWS
    for h in range(A_HEADS):
        vt_ref[h * rows:h * rows + dh, :] = vt[h * dh:(h + 1) * dh, :]
        vt_ref[h * rows + dh:(h + 1) * rows, :] = jnp.ones((rows - dh, vt.shape[1]), BF16)


def _moba_qkv(x, w_qk, w_vt, batch, seq):
    t = x.shape[0]
    tm = ROW_TILE
    tps = seq // tm
    return pl.pallas_call(
        _moba_qkv_kernel,
        grid=(t // tm,),
        in_specs=[
            pl.BlockSpec((tm, D_MODEL), lambda i: (i, 0)),
            _const_spec((D_MODEL, 2 * D_MODEL)),
            _const_spec((D_MODEL, D_MODEL)),
        ],
        out_specs=[
            pl.BlockSpec((tm, 2 * D_MODEL), lambda i: (i, 0)),
            pl.BlockSpec((None, A_HEADS * MOBA_VT_ROWS, tm), lambda i: (i // tps, 0, i % tps)),
        ],
        out_shape=[
            jax.ShapeDtypeStruct((t, 2 * D_MODEL), BF16),
            jax.ShapeDtypeStruct((batch, A_HEADS * MOBA_VT_ROWS, seq), BF16),
        ],
        compiler_params=_params("parallel"),
        name="moba_qkv",
    )(x, w_qk, w_vt)


def _moba_kernel(q_ref, qn_ref, k_ref, vt_ref, tab_ref, o_ref, kmean_ref, sel_ref, t_ref, p_ref, m_ref, *, nb):
    blk = MOBA_BLOCK
    dh = A_HEAD_DIM
    gb = MOBA_GROUP
    rows = MOBA_VT_ROWS
    sub = MOBA_EXP_ROWS
    heads = range(MOBA_HEADS_PER_STEP)
    s = pl.program_id(2)

    @pl.when(s == 0)
    def _():
        for hh in heads:
            for j in range(nb):
                kb = k_ref[j * blk:(j + 1) * blk, hh * dh:(hh + 1) * dh].astype(F32)
                kmean_ref[hh, j:j + 1, :] = jnp.mean(kb, axis=0, keepdims=True)
        o_ref[...] = jnp.zeros_like(o_ref)
        own = lax.broadcasted_iota(jnp.int32, (nb, blk), 0) == 0
        for hh in heads:
            sel_ref[hh] = jnp.where(own, 1.0, 0.0)

    def select_next():
        i = jnp.minimum(s + 1, nb - 1)
        jidx = lax.broadcasted_iota(jnp.int32, (nb, blk), 0)
        for hh in heads:
            q = qn_ref[:, hh * dh:(hh + 1) * dh]
            gate = _dot_nt(kmean_ref[hh].astype(BF16), q)
            cnt = jnp.zeros((nb, blk), F32)
            for jp in range(nb - 1):
                row = gate[jp:jp + 1, :]
                beats = (row > gate) | ((row == gate) & (jp < jidx))
                cnt = cnt + jnp.where(beats & (jp < i), 1.0, 0.0)
            chosen = ((cnt < MOBA_TOPK) & (jidx < i)) | (jidx == i)
            sel_ref[hh] = jnp.where(chosen, 1.0, 0.0)

    def step(ng_score, ng_finish):
        first_dynamic = (ng_score - 1) * gb - 1
        nk_finish = ng_finish * gb * blk
        if ng_finish:
            m_prev = [m_ref[hh] for hh in heads]
        m8 = [None] * len(heads)
        for j in range(max(ng_score, ng_finish) * gb):
            if j < ng_finish * gb:
                for hh in heads:
                    for r in range(j * blk, (j + 1) * blk, sub):
                        p_ref[hh, r:r + sub, :] = jnp.exp2(t_ref[hh, r:r + sub, :] - m_prev[hh]).astype(BF16)
            if j < ng_score * gb:
                for hh in heads:
                    t = _dot_nt(k_ref[j * blk:(j + 1) * blk, hh * dh:(hh + 1) * dh],
                                q_ref[:, hh * dh:(hh + 1) * dh])
                    if j >= first_dynamic:
                        t = t + tab_ref[hh, jnp.clip(s - j, 0, 2)]
                    t = jnp.where(sel_ref[hh, j:j + 1, :] > 0.5, t, NEG)
                    t_ref[hh, j * blk:(j + 1) * blk, :] = t
                    mb = jnp.max(t.reshape(blk // 8, 8, blk), axis=0)
                    m8[hh] = mb if j == 0 else jnp.maximum(m8[hh], mb)
        if ng_score:
            for hh in heads:
                m_ref[hh] = jnp.max(m8[hh], axis=0, keepdims=True)
        select_next()
        if ng_finish:
            for hh in heads:
                o_aug = _dot(vt_ref[hh * rows:(hh + 1) * rows, 0:nk_finish], p_ref[hh, 0:nk_finish, :])
                o_t = o_aug[0:dh, :] / o_aug[dh:dh + 1, :]
                o_ref[:, hh * dh:(hh + 1) * dh] = jnp.transpose(o_t).astype(o_ref.dtype)

    ng_score = jnp.where(s < nb, s // gb + 1, 0)
    ng_finish = jnp.where(s >= 1, (s - 1) // gb + 1, 0)
    combos = sorted({(q // gb + 1 if q < nb else 0, (q - 1) // gb + 1 if q >= 1 else 0) for q in range(nb + 1)})
    for a, b in combos:
        pl.when((ng_score == a) & (ng_finish == b))(functools.partial(step, a, b))


def _moba_attention(qk, vt, tab, batch, seq):
    blk = MOBA_BLOCK
    nb = seq // blk
    hp = MOBA_HEADS_PER_STEP
    w = hp * A_HEAD_DIM
    ngrp = A_HEADS // hp
    assert seq % blk == 0 and nb % MOBA_GROUP == 0 and A_HEADS % hp == 0
    kern = functools.partial(_moba_kernel, nb=nb)
    return pl.pallas_call(
        kern,
        grid=(batch, ngrp, nb + 1),
        in_specs=[
            pl.BlockSpec((blk, w), lambda b, h, s: (b * nb + jnp.minimum(s, nb - 1), h)),
            pl.BlockSpec((blk, w), lambda b, h, s: (b * nb + jnp.minimum(s + 1, nb - 1), h)),
            pl.BlockSpec((seq, w), lambda b, h, s: (b, ngrp + h)),
            pl.BlockSpec((None, hp * MOBA_VT_ROWS, seq), lambda b, h, s: (b, h, 0)),
            pl.BlockSpec((hp, 3, blk, blk), lambda b, h, s: (h, 0, 0, 0)),
        ],
        out_specs=pl.BlockSpec((blk, w), lambda b, h, s: (b * nb + jnp.maximum(s - 1, 0), h)),
        out_shape=jax.ShapeDtypeStruct((batch * seq, D_MODEL), BF16),
        scratch_shapes=[
            pltpu.VMEM((hp, nb, A_HEAD_DIM), F32),
            pltpu.VMEM((hp, nb, blk), F32),
            pltpu.VMEM((hp, seq, blk), F32),
            pltpu.VMEM((hp, seq, blk), BF16),
            pltpu.VMEM((hp, 1, blk), F32),
        ],
        compiler_params=_params("parallel", "parallel", "arbitrary"),
        name="moba_attn",
    )(qk, qk, qk, vt, tab)


def _moba_mixer(x, w_in, rel_bias, batch, seq):
    assert REL_MAX_DIST <= MOBA_BLOCK
    d = D_MODEL
    c1 = (A_HEAD_DIM ** -0.5) * LOG2E
    w_qk = jnp.concatenate([w_in[:, :d] * c1, w_in[:, d:2 * d]], axis=1).astype(BF16)
    w_vt = w_in[:, 2 * d:].T.astype(BF16)
    qk, vt = _moba_qkv(x, w_qk, w_vt, batch, seq)
    return _moba_attention(qk, vt, _moba_bias_tables(rel_bias), batch, seq)


def _pool_kernel(x_ref, halo_ref, win_ref, wgrp_ref, scale_ref, wout_ref, g_ref, b_ref, o_ref,
                 ubuf_ref, ybuf_ref, *, tiles_per_seq):
    tm = x_ref.shape[0]
    hl = POOL_HALO
    ti = pl.program_id(0) % tiles_per_seq
    x = x_ref[...]
    u_halo = _dot(halo_ref[...].astype(BF16), win_ref[...])
    ubuf_ref[0:hl, :] = jnp.where(ti == 0, 0.0, u_halo)
    ubuf_ref[hl:hl + tm, :] = _dot(x.astype(BF16), win_ref[...])
    pos = ti * tm + lax.broadcasted_iota(jnp.int32, (tm, POOL_GROUP), 0)
    for gi, w in enumerate(POOL_WINDOWS):
        lo = gi * POOL_GROUP
        u = ubuf_ref[hl:hl + tm, lo:lo + POOL_GROUP]
        acc = ubuf_ref[:, lo:lo + POOL_GROUP]
        d = 1
        while d < w:
            acc = acc + pltpu.roll(acc, d, 0)
            d *= 2
        ws = acc[hl:hl + tm, :]
        cnt = jnp.minimum(pos + 1, w).astype(F32)
        pooled = ws / cnt - u
        yg = _dot(pooled.astype(BF16), wgrp_ref[gi]) * scale_ref[:, lo:lo + POOL_GROUP]
        ybuf_ref[:, lo:lo + POOL_GROUP] = yg.astype(BF16)
    y = _dot(ybuf_ref[...], wout_ref[...])
    o_ref[...] = _layer_norm(ALPHA * x + y, g_ref[...], b_ref[...])


def _pool_layer(x, w_in, w_group, scale, w_out, g, b, seq):
    t = x.shape[0]
    tm = ROW_TILE
    hl = POOL_HALO
    assert seq % tm == 0 and tm % hl == 0 and max(POOL_WINDOWS) <= hl
    kern = functools.partial(_pool_kernel, tiles_per_seq=seq // tm)
    ng = len(POOL_WINDOWS)
    return pl.pallas_call(
        kern,
        grid=(t // tm,),
        in_specs=[
            pl.BlockSpec((tm, D_MODEL), lambda i: (i, 0)),
            pl.BlockSpec((hl, D_MODEL), lambda i: (jnp.maximum(i * (tm // hl) - 1, 0), 0)),
            _const_spec((D_MODEL, D_MODEL)),
            _const_spec((ng, POOL_GROUP, POOL_GROUP)),
            _const_spec((1, D_MODEL)),
            _const_spec((D_MODEL, D_MODEL)),
            _const_spec((1, D_MODEL)),
            _const_spec((1, D_MODEL)),
        ],
        out_specs=pl.BlockSpec((tm, D_MODEL), lambda i: (i, 0)),
        out_shape=jax.ShapeDtypeStruct((t, D_MODEL), F32),
        scratch_shapes=[pltpu.VMEM((tm + hl, D_MODEL), F32), pltpu.VMEM((tm, D_MODEL), BF16)],
        compiler_params=_params("parallel"),
        name="pool_layer",
    )(x, x, w_in, w_group, scale, w_out, g, b)


def _log_sigmoid(x):
    return jnp.minimum(x, 0.0) - jnp.log1p(jnp.exp(-jnp.abs(x)))


def _mlstm_kernel(xn_ref, xc_ref, win_ref, wg_ref, wgt_ref, bg_row_ref, bg_col_ref, cw_ref,
                  ng_ref, wout_ref, g_ref, b_ref, o_ref,
                  c_ref, n_ref, m_ref, cprev_ref, cbuf_ref, hn_ref, pa_ref, pb_ref):
    c = pl.program_id(1)

    @pl.when(c == 0)
    def _():
        pb_ref[...] = jnp.zeros_like(pb_ref)

    @pl.when(c <= 1)
    def _():
        c_ref[...] = jnp.zeros_like(c_ref)
        n_ref[...] = jnp.zeros_like(n_ref)
        m_ref[...] = jnp.zeros_like(m_ref)
        cprev_ref[...] = jnp.zeros_like(cprev_ref)

    pl.when(c % 2 == 0)(functools.partial(
        _mlstm_step, pa_ref, pb_ref, xn_ref, xc_ref, win_ref, wg_ref, wgt_ref, bg_row_ref, bg_col_ref, cw_ref,
        ng_ref, wout_ref, g_ref, b_ref, o_ref, c_ref, n_ref, m_ref, cprev_ref, cbuf_ref, hn_ref))
    pl.when(c % 2 == 1)(functools.partial(
        _mlstm_step, pb_ref, pa_ref, xn_ref, xc_ref, win_ref, wg_ref, wgt_ref, bg_row_ref, bg_col_ref, cw_ref,
        ng_ref, wout_ref, g_ref, b_ref, o_ref, c_ref, n_ref, m_ref, cprev_ref, cbuf_ref, hn_ref))


def _mlstm_step(p_write, p_read, xn_ref, xc_ref, win_ref, wg_ref, wgt_ref, bg_row_ref, bg_col_ref, cw_ref,
                ng_ref, wout_ref, g_ref, b_ref, o_ref, c_ref, n_ref, m_ref, cprev_ref, cbuf_ref, hn_ref):
    L = C_CHUNK
    d = D_MODEL
    dh = C_HEAD_DIM
    nh = C_HEADS
    hl = CONV_HALO
    pw = 4 * d // nh

    xnb = xn_ref[...].astype(BF16)
    x = xc_ref[...]
    xb = x.astype(BF16)
    g_col = _dot(xb, wg_ref[...]) + bg_row_ref[...]
    g_row = _dot_nt(wgt_ref[...], xb) + bg_col_ref[...]

    cbuf_ref[0:hl, :] = cprev_ref[...]
    cbuf_ref[hl:hl + L, :] = p_read[:, 0:2 * d]
    cprev_ref[...] = p_read[L - hl:L, 0:2 * d]
    conv = cw_ref[C_CONV - 1:C_CONV, :] * p_read[:, 0:2 * d]
    for j in range(C_CONV - 2, -1, -1):
        off = hl - (C_CONV - 1) + j
        conv = conv + cw_ref[j:j + 1, :] * cbuf_ref[off:off + L, :]
    qk = conv * _sigmoid(conv)

    r = lax.broadcasted_iota(jnp.int32, (L, L), 0)
    cc = lax.broadcasted_iota(jnp.int32, (L, L), 1)
    lower = cc <= r

    for hd in range(nh):
        p_write[:, hd * pw:(hd + 1) * pw] = _dot(xnb, win_ref[:, hd * pw:(hd + 1) * pw])

        i_col = g_col[:, hd:hd + 1]
        i_row = g_row[hd:hd + 1, :]
        lf_col = _log_sigmoid(g_col[:, nh + hd:nh + hd + 1])
        lf_row = _log_sigmoid(g_row[nh + hd:nh + hd + 1, :])
        b_col = jnp.sum(jnp.where(lower, lf_row, 0.0), axis=1, keepdims=True)
        b_row = jnp.sum(jnp.where(r <= cc, lf_col, 0.0), axis=0, keepdims=True)
        b_last = jnp.sum(lf_row, axis=1, keepdims=True)

        q_h = qk[:, hd * dh:(hd + 1) * dh]
        k_h = qk[:, d + hd * dh:d + (hd + 1) * dh] * (dh ** -0.5)
        qb = q_h.astype(BF16)
        kb = k_h.astype(BF16)
        vb = p_read[:, 2 * d + hd * dh:2 * d + (hd + 1) * dh].astype(BF16)
        c_st = c_ref[hd]
        n_st = n_ref[hd]
        m_prev = m_ref[hd]

        d_intra = jnp.where(lower, b_col - b_row + i_row, NEG)
        m_inter = b_col + m_prev
        m_t = jnp.maximum(m_inter, jnp.max(d_intra, axis=1, keepdims=True))
        w = jnp.exp(d_intra - m_t) * _dot_nt(qb, kb)
        s_inter = jnp.exp(m_inter - m_t)
        num = s_inter * _dot(qb, c_st.astype(BF16)) + _dot(w.astype(BF16), vb)
        den = s_inter * jnp.sum(q_h * n_st, axis=1, keepdims=True) + jnp.sum(w, axis=1, keepdims=True)
        ht = num / jnp.maximum(jnp.abs(den), jnp.exp(-m_t))

        gg_col = b_last - b_col + i_col
        gg_row = b_last - b_row + i_row
        m_new = jnp.maximum(b_last + m_prev, jnp.max(gg_row, axis=1, keepdims=True))
        decay = jnp.exp(b_last + m_prev - m_new)
        kw = k_h * jnp.exp(gg_col - m_new)
        c_ref[hd] = decay * c_st + _dot(jnp.transpose(kw).astype(BF16), vb)
        n_ref[hd] = decay * n_st + jnp.sum(kw, axis=0, keepdims=True)
        m_ref[hd] = m_new

        hc = _sigmoid(p_read[:, 3 * d + hd * dh:3 * d + (hd + 1) * dh]) * ht
        mu = jnp.mean(hc, axis=1, keepdims=True)
        hcc = hc - mu
        var = jnp.mean(hcc * hcc, axis=1, keepdims=True)
        hn = hcc * lax.rsqrt(var + LN_EPS) * ng_ref[:, hd * dh:(hd + 1) * dh]
        hn_ref[:, hd * dh:(hd + 1) * dh] = hn.astype(BF16)

    y = _dot(hn_ref[...], wout_ref[...])
    o_ref[...] = _layer_norm(ALPHA * x + y, g_ref[...], b_ref[...])


def _mlstm_layer(x, w_in, b_gates, conv_w, norm_g, w_out, g, b, batch, seq):
    t = x.shape[0]
    L = C_CHUNK
    nc = seq // L
    d = D_MODEL
    nh = C_HEADS
    assert seq % L == 0
    w_gate = w_in[:, 4 * d:]
    wg = jnp.pad(w_gate, ((0, 0), (0, GATE_PAD - 2 * nh)))
    wgt = jnp.pad(w_gate.T, ((0, 16 - 2 * nh), (0, 0)))
    bg_row = jnp.pad(b_gates, (0, GATE_PAD - 2 * nh))[None, :].astype(F32)
    bg_col = jnp.pad(b_gates, (0, 16 - 2 * nh))[:, None].astype(F32)
    return pl.pallas_call(
        _mlstm_kernel,
        grid=(batch, nc + 1),
        in_specs=[
            pl.BlockSpec((L, d), lambda bb, c: (bb * nc + jnp.minimum(c, nc - 1), 0)),
            pl.BlockSpec((L, d), lambda bb, c: (bb * nc + jnp.maximum(c - 1, 0), 0)),
            pl.BlockSpec((d, 4 * d), lambda bb, c: (0, 0), pipeline_mode=pl.Buffered(1)),
            _const_spec((d, GATE_PAD)),
            _const_spec((16, d)),
            _const_spec((1, GATE_PAD)),
            _const_spec((16, 1)),
            _const_spec((C_CONV, 2 * d)),
            _const_spec((1, d)),
            _const_spec((d, d)),
            _const_spec((1, d)),
            _const_spec((1, d)),
        ],
        out_specs=pl.BlockSpec((L, d), lambda bb, c: (bb * nc + jnp.maximum(c - 1, 0), 0)),
        out_shape=jax.ShapeDtypeStruct((t, d), F32),
        scratch_shapes=[
            pltpu.VMEM((nh, C_HEAD_DIM, C_HEAD_DIM), F32),
            pltpu.VMEM((nh, 1, C_HEAD_DIM), F32),
            pltpu.VMEM((nh, 1, 1), F32),
            pltpu.VMEM((CONV_HALO, 2 * d), F32),
            pltpu.VMEM((CONV_HALO + L, 2 * d), F32),
            pltpu.VMEM((L, d), BF16),
            pltpu.VMEM((L, 4 * d), F32),
            pltpu.VMEM((L, 4 * d), F32),
        ],
        compiler_params=_params("arbitrary", "arbitrary"),
        name="mlstm_layer",
    )(x, x, w_in, wg.astype(BF16), wgt.astype(BF16), bg_row, bg_col, conv_w.astype(F32),
      norm_g[None, :].astype(F32), w_out, g, b)


def kernel(x, rel_bias, ln_g, ln_b, ffn_w_gu, ffn_w_down, a_w_in, a_w_out, b_w_in, b_w_group, b_scale, b_w_out,
           c_w_in, c_b_gates, c_conv_w, c_norm_g, c_w_out):
    batch, seq, d = x.shape
    h = x.reshape(batch * seq, d)
    bf = lambda w: w.astype(BF16)
    for i in range(DEPTH):
        lg = lambda s: ln_g[i, s][None, :]
        lb = lambda s: ln_b[i, s][None, :]
        h = _ffn(h, ffn_w_gu, ffn_w_down, i, 0, lg(0), lb(0))
        kind, j = i % N_MIXERS, i // N_MIXERS
        pre = None
        if kind == 0:
            pre = (_moba_mixer(h, a_w_in[j], rel_bias, batch, seq), bf(a_w_out[j]), lg(1), lb(1))
        elif kind == 1:
            h = _pool_layer(h, bf(b_w_in[j]), bf(b_w_group[j]), b_scale[j][None, :], bf(b_w_out[j]),
                            lg(1), lb(1), seq)
        else:
            h = _mlstm_layer(h, bf(c_w_in[j]), c_b_gates[j], c_conv_w[j], c_norm_g[j], bf(c_w_out[j]),
                             lg(1), lb(1), batch, seq)
        h = _ffn(h, ffn_w_gu, ffn_w_down, i, 1, lg(2), lb(2), pre)
    return h.reshape(batch, seq, d)
```
